```python
import math
import jax
import jax.numpy as jnp
from jax import lax
import numpy as np

D_MODEL = 1024
BATCH = 8
SEQ = 4096
DEPTH = 2

CHUNK = 64
D_MIX = D_MODEL
EPS = 1e-6

GM_WIDTH = D_MIX // 4
GM_HEADS = 4
GM_HEAD_DIM = GM_WIDTH // GM_HEADS
GM_BLOCK = 128

CV_WIDTH = D_MIX // 4
CV_KERNEL = 31

DN_WIDTH = D_MIX // 2
DN_HEAD_DIM = 128
DN_HEADS = DN_WIDTH // DN_HEAD_DIM
DN_CONV = 4

D_FF = -(-8 * D_MODEL // (3 * 256)) * 256

SPLIT_SIZES = (GM_WIDTH, GM_WIDTH, CV_WIDTH, CV_WIDTH,
               DN_WIDTH, DN_WIDTH, DN_WIDTH, DN_WIDTH, DN_HEADS, DN_HEADS)
D_PROJ = 2 * GM_WIDTH + 2 * CV_WIDTH + 4 * DN_WIDTH + 2 * DN_HEADS

kernel_name = "hybrid_gmlp_conv_deltanet_block"


def _split_offsets(sizes):
    out, s = [], 0
    for n in sizes[:-1]:
        s += n
        out.append(s)
    return out


def rms_norm(x, g):
    xf = x.astype(jnp.float32)
    y = xf * lax.rsqrt(jnp.mean(xf * xf, axis=-1, keepdims=True) + EPS)
    return (y * g.astype(jnp.float32)).astype(x.dtype)


def layer_norm(x, g, b):
    xf = x.astype(jnp.float32)
    mu = jnp.mean(xf, axis=-1, keepdims=True)
    var = jnp.mean(jnp.square(xf - mu), axis=-1, keepdims=True)
    y = (xf - mu) * lax.rsqrt(var + EPS)
    return (y * g.astype(jnp.float32) + b.astype(jnp.float32)).astype(x.dtype)


def causal_depthwise_conv(x, w):
    k = w.shape[0]
    return lax.conv_general_dilated(
        x, w[:, None, :].astype(x.dtype), window_strides=(1,), padding=[(k - 1, 0)],
        dimension_numbers=("NWC", "WIO", "NWC"), feature_group_count=x.shape[-1])


def spatial_gating(u, v, ln_g, ln_b, ws, bs):
    b, t, _ = v.shape
    v = layer_norm(v, ln_g, ln_b)
    causal = jnp.tril(jnp.ones((GM_BLOCK, GM_BLOCK), dtype=bool))
    ws = jnp.where(causal, ws, jnp.zeros_like(ws))
    vb = v.reshape(b, t // GM_BLOCK, GM_BLOCK, GM_HEADS, GM_HEAD_DIM)
    s = jnp.einsum("hij,bnjhd->bnihd", ws, vb) + bs.T[:, :, None]
    return u * s.reshape(b, t, GM_WIDTH)


def conformer_conv(a, gate, dw_w, dw_b, ln_g, ln_b):
    y = a * jax.nn.sigmoid(gate)
    y = causal_depthwise_conv(y, dw_w) + dw_b
    y = layer_norm(y, ln_g, ln_b)
    return jax.nn.silu(y)


def l2_normalize(x):
    return x * lax.rsqrt(jnp.sum(x * x, axis=-1, keepdims=True) + EPS)


def chunked_gated_delta_rule(q, k, v, beta, log_alpha):
    b, t, h, dk = q.shape
    dv = v.shape[-1]
    n = t // CHUNK

    def chunks(z):
        return z.reshape(b, n, CHUNK, h, -1).transpose(0, 3, 1, 2, 4)

    def chunks_scalar(z):
        return z.reshape(b, n, CHUNK, h).transpose(0, 3, 1, 2)

    qc, kc, vc = chunks(q), chunks(k), chunks(v)
    bc = chunks_scalar(beta)
    gam = jnp.cumsum(chunks_scalar(log_alpha), axis=-1)

    diff = gam[..., :, None] - gam[..., None, :]
    strict = jnp.tril(jnp.ones((CHUNK, CHUNK), dtype=bool), -1)
    incl = jnp.tril(jnp.ones((CHUNK, CHUNK), dtype=bool))
    decay_strict = jnp.exp(jnp.where(strict, diff, -jnp.inf))
    decay_incl = jnp.exp(jnp.where(incl, diff, -jnp.inf))

    kk = jnp.einsum("bhnid,bhnjd->bhnij", kc, kc)
    lower = jnp.eye(CHUNK, dtype=jnp.float32) + bc[..., :, None] * kk * decay_strict
    rhs = jnp.concatenate([bc[..., None] * vc, (bc * jnp.exp(gam))[..., None] * kc], axis=-1)
    sol = lax.linalg.triangular_solve(lower, rhs, left_side=True, lower=True)
    u_part, w_part = sol[..., :dv], sol[..., dv:]

    qk = jnp.einsum("bhnid,bhnjd->bhnij", qc, kc) * decay_incl
    q_dec = qc * jnp.exp(gam)[..., None]
    k_dec = kc * jnp.exp(gam[..., -1:] - gam)[..., None]
    chunk_decay = jnp.exp(gam[..., -1])

    def step(state, inp):
        u_n, w_n, qk_n, qd_n, kd_n, cd_n = inp
        u = u_n - jnp.einsum("bhck,bhkv->bhcv", w_n, state)
        o = jnp.einsum("bhck,bhkv->bhcv", qd_n, state) + jnp.einsum("bhij,bhjv->bhiv", qk_n, u)
        state = cd_n[..., None, None] * state + jnp.einsum("bhck,bhcv->bhkv", kd_n, u)
        return state, o

    xs = tuple(jnp.moveaxis(z, 2, 0) for z in (u_part, w_part, qk, q_dec, k_dec, chunk_decay))
    s0 = jnp.zeros((b, h, dk, dv), jnp.float32)
    _, o = lax.scan(step, s0, xs)
    return o.transpose(1, 0, 3, 2, 4).reshape(b, t, h, dv)


def gated_deltanet(q, k, v, gate, beta_logit, alpha_logit, conv_w, a_log, dt_bias, norm_g):
    b, t, _ = q.shape
    dtype = q.dtype
    qkv = jax.nn.silu(causal_depthwise_conv(jnp.concatenate([q, k, v], axis=-1), conv_w))
    q, k, v = jnp.split(qkv.astype(jnp.float32), 3, axis=-1)
    q = l2_normalize(q.reshape(b, t, DN_HEADS, DN_HEAD_DIM)) * (DN_HEAD_DIM ** -0.5)
    k = l2_normalize(k.reshape(b, t, DN_HEADS, DN_HEAD_DIM))
    v = v.reshape(b, t, DN_HEADS, DN_HEAD_DIM)
    beta = jax.nn.sigmoid(beta_logit.astype(jnp.float32))
    log_alpha = -jnp.exp(a_log.astype(jnp.float32)) * jax.nn.softplus(
        alpha_logit.astype(jnp.float32) + dt_bias.astype(jnp.float32))
    o = chunked_gated_delta_rule(q, k, v, beta, log_alpha)
    g = gate.astype(jnp.float32).reshape(b, t, DN_HEADS, DN_HEAD_DIM)
    o = rms_norm(o, norm_g) * jax.nn.silu(g)
    return o.reshape(b, t, DN_WIDTH).astype(dtype)


def setup_inputs(seed: int = 0) -> dict:
    key = jax.random.key(seed)
    ks = jax.random.split(key, 22)
    f32 = jnp.float32
    L = DEPTH

    def normal(k, shape, scale):
        return scale * jax.random.normal(k, shape, f32)

    def gain(k, shape):
        return 1.0 + 0.05 * jax.random.normal(k, shape, f32)

    dt = jnp.exp(jax.random.uniform(ks[13], (L, DN_HEADS), f32, math.log(1e-3), math.log(1e-1)))
    return {
        "x": normal(ks[0], (BATCH, SEQ, D_MODEL), 1.0),
        "mix_norm_g": gain(ks[1], (L, D_MODEL)),
        "w_in": normal(ks[2], (L, D_MODEL, D_PROJ), D_MODEL ** -0.5),
        "gm_ln_g": gain(ks[3], (L, GM_WIDTH)),
        "gm_ln_b": normal(ks[4], (L, GM_WIDTH), 0.02),
        "gm_ws": normal(ks[5], (L, GM_HEADS, GM_BLOCK, GM_BLOCK), GM_BLOCK ** -0.5),
        "gm_bs": 1.0 + normal(ks[6], (L, GM_HEADS, GM_BLOCK), 0.02),
        "cv_dw_w": normal(ks[7], (L, CV_KERNEL, CV_WIDTH), CV_KERNEL ** -0.5),
        "cv_dw_b": normal(ks[8], (L, CV_WIDTH), 0.02),
        "cv_ln_g": gain(ks[9], (L, CV_WIDTH)),
        "cv_ln_b": normal(ks[10], (L, CV_WIDTH), 0.02),
        "dn_conv_w": normal(ks[11], (L, DN_CONV, 3 * DN_WIDTH), DN_CONV ** -0.5),
        "dn_a_log": jnp.log(jax.random.uniform(ks[12], (L, DN_HEADS), f32, 1.0, 16.0)),
        "dn_dt_bias": dt + jnp.log(-jnp.expm1(-dt)),
        "dn_norm_g": gain(ks[14], (L, DN_HEAD_DIM)),
        "w_out": normal(ks[15], (L, D_MIX, D_MODEL), D_MIX ** -0.5),
        "ffn_norm_g": gain(ks[16], (L, D_MODEL)),
        "w_gate": normal(ks[17], (L, D_MODEL, D_FF), D_MODEL ** -0.5),
        "w_up": normal(ks[18], (L, D_MODEL, D_FF), D_MODEL ** -0.5),
        "w_down": normal(ks[19], (L, D_FF, D_MODEL), D_FF ** -0.5),
        "final_norm_g": gain(ks[20], (D_MODEL,)),
    }


def reference(x, mix_norm_g, w_in, gm_ln_g, gm_ln_b, gm_ws, gm_bs, cv_dw_w, cv_dw_b,
              cv_ln_g, cv_ln_b, dn_conv_w, dn_a_log, dn_dt_bias, dn_norm_g, w_out,
              ffn_norm_g, w_gate, w_up, w_down, final_norm_g):
    offsets = _split_offsets(SPLIT_SIZES)
    for l in range(DEPTH):
        h = rms_norm(x, mix_norm_g[l])
        p = jnp.einsum("btd,de->bte", h, w_in[l])
        (gm_u, gm_v, cv_a, cv_g, dn_q, dn_k, dn_v, dn_g,
         dn_beta, dn_alpha) = jnp.split(p, offsets, axis=-1)
        y_a = spatial_gating(jax.nn.gelu(gm_u, approximate=False), jax.nn.gelu(gm_v, approximate=False),
                             gm_ln_g[l], gm_ln_b[l], gm_ws[l], gm_bs[l])
        y_b = conformer_conv(cv_a, cv_g, cv_dw_w[l], cv_dw_b[l], cv_ln_g[l], cv_ln_b[l])
        y_c = gated_deltanet(dn_q, dn_k, dn_v, dn_g, dn_beta, dn_alpha, dn_conv_w[l],
                             dn_a_log[l], dn_dt_bias[l], dn_norm_g[l])
        mix = jnp.concatenate([y_a, y_b, y_c], axis=-1)
        x = x + jnp.einsum("bte,ed->btd", mix, w_out[l])
        h = rms_norm(x, ffn_norm_g[l])
        ff = jax.nn.silu(jnp.einsum("btd,df->btf", h, w_gate[l])) * jnp.einsum("btd,df->btf", h, w_up[l])
        x = x + jnp.einsum("btf,fd->btd", ff, w_down[l])
    return rms_norm(x, final_norm_g)
```

```python
import functools

import jax
import jax.numpy as jnp
from jax import lax
from jax.experimental import pallas as pl
from jax.experimental.pallas import tpu as pltpu

F32 = jnp.float32
BF16 = jnp.bfloat16
EPS = 1e-6

LANES = 128
V7X_VMEM_BYTES = 64 * 1024 * 1024

GM_WIDTH = 256
GM_HEADS = 4
GM_HEAD_DIM = GM_WIDTH // GM_HEADS
GM_BLOCK = 128
CV_WIDTH = 256
CV_KERNEL = 31
DN_WIDTH = 512
DN_HEAD_DIM = 128
DN_HEADS = 4
DN_CONV = 4
CHUNK = 64

OFF_GM_U = 0
OFF_GM_V = OFF_GM_U + GM_WIDTH
OFF_CV_A = OFF_GM_V + GM_WIDTH
OFF_CV_G = OFF_CV_A + CV_WIDTH
OFF_DN_QKV = OFF_CV_G + CV_WIDTH
OFF_DN_GATE = OFF_DN_QKV + 3 * DN_WIDTH
OFF_DN_BA = OFF_DN_GATE + DN_WIDTH
D_PROJ_PAD = OFF_DN_BA + LANES

TM_IN = 512
TM_FFN = 512
TB_MIX = 512
FF_CHUNK = 256
CV_HALO = 32
DN_HALO = 8
CV_ROWS = 64
DN_ROWS = 128


def _rms_norm(x, g):
    ms = jnp.mean(x * x, axis=-1, keepdims=True)
    return x * lax.rsqrt(ms + EPS) * g


def _layer_norm(x, g, b):
    mu = jnp.mean(x, axis=-1, keepdims=True)
    xc = x - mu
    var = jnp.mean(xc * xc, axis=-1, keepdims=True)
    return xc * lax.rsqrt(var + EPS) * g + b


def _gelu(x):
    return 0.5 * x * (1.0 + lax.erf(x * (2.0 ** -0.5)))


def _dot(a, b):
    return jnp.dot(a.astype(BF16), b.astype(BF16), preferred_element_type=F32)


def _dot_nt(a, b):
    return lax.dot_general(a.astype(BF16), b.astype(BF16), (((1,), (1,)), ((), ())),
                           preferred_element_type=F32)


def _dot_tn(a, b):
    return lax.dot_general(a.astype(BF16), b.astype(BF16), (((0,), (0,)), ((), ())),
                           preferred_element_type=F32)


def _in_proj_kernel(x_ref, g_ref, w_ref, o_ref):
    h = _rms_norm(x_ref[...], g_ref[...]).astype(BF16)
    n = w_ref.shape[1]
    step = 512
    for c0 in range(0, n, step):
        c1 = min(c0 + step, n)
        o_ref[:, c0:c1] = jnp.dot(h, w_ref[:, c0:c1], preferred_element_type=F32)


def _in_proj(x2d, g, w_pad):
    n_tok, d = x2d.shape
    n_out = w_pad.shape[1]
    return pl.pallas_call(
        _in_proj_kernel,
        grid=(n_tok // TM_IN,),
        in_specs=[
            pl.BlockSpec((TM_IN, d), lambda i: (i, 0)),
            pl.BlockSpec((1, d), lambda i: (0, 0)),
            pl.BlockSpec((d, n_out), lambda i: (0, 0)),
        ],
        out_specs=pl.BlockSpec((TM_IN, n_out), lambda i: (i, 0)),
        out_shape=jax.ShapeDtypeStruct((n_tok, n_out), F32),
        compiler_params=pltpu.CompilerParams(
            dimension_semantics=("arbitrary",),
            vmem_limit_bytes=48 * 1024 * 1024),
    )(x2d, g, w_pad)


def _out_ffn_kernel(x_ref, mix_ref, wo_ref, g_ref, wgu_ref, wd_ref, gf_ref, o_ref,
                    acc_ref, h_ref, *, final_norm):
    xn = x_ref[...] + jnp.dot(mix_ref[...], wo_ref[...], preferred_element_type=F32)
    acc_ref[...] = xn
    h_ref[...] = _rms_norm(xn, g_ref[...]).astype(BF16)

    def body(j, carry):
        gu = jnp.dot(h_ref[...], wgu_ref[j], preferred_element_type=F32)
        gate = gu[:, :FF_CHUNK]
        up = gu[:, FF_CHUNK:]
        act = (gate * jax.nn.sigmoid(gate) * up).astype(BF16)
        acc_ref[...] += jnp.dot(act, wd_ref[j], preferred_element_type=F32)
        return carry

    lax.fori_loop(0, wgu_ref.shape[0], body, 0)
    y = acc_ref[...]
    if final_norm:
        y = _rms_norm(y, gf_ref[...])
    o_ref[...] = y


def _out_ffn(x2d, mix2d, wo, g, wgu, wd, gf, final_norm):
    n_tok, d = x2d.shape
    nch = wgu.shape[0]
    const2 = lambda i: (0, 0)
    const3 = lambda i: (0, 0, 0)
    return pl.pallas_call(
        functools.partial(_out_ffn_kernel, final_norm=final_norm),
        grid=(n_tok // TM_FFN,),
        in_specs=[
            pl.BlockSpec((TM_FFN, d), lambda i: (i, 0)),
            pl.BlockSpec((TM_FFN, mix2d.shape[1]), lambda i: (i, 0)),
            pl.BlockSpec(wo.shape, const2),
            pl.BlockSpec((1, d), const2),
            pl.BlockSpec(wgu.shape, const3),
            pl.BlockSpec(wd.shape, const3),
            pl.BlockSpec((1, d), const2),
        ],
        out_specs=pl.BlockSpec((TM_FFN, d), lambda i: (i, 0)),
        out_shape=jax.ShapeDtypeStruct((n_tok, d), F32),
        scratch_shapes=[pltpu.VMEM((TM_FFN, d), F32), pltpu.VMEM((TM_FFN, d), BF16)],
        compiler_params=pltpu.CompilerParams(
            dimension_semantics=("arbitrary",),
            vmem_limit_bytes=56 * 1024 * 1024),
    )(x2d, mix2d, wo, g, wgu, wd, gf)


def _lane_bcast(x, col, rows, width):
    return jnp.broadcast_to(x[:, col:col + 1], (rows, width))


def _mix_kernel(p_ref, gm_g_ref, gm_b_ref, ws_ref, gm_bias_ref,
                cvw_ref, cvb_ref, cvg_ref, cvbeta_ref,
                dnw_ref, alog_ref, dtb_ref, dng_ref,
                mix_ref,
                ypad_ref, dnpad_ref, qkv_ref, beta_ref, la_ref, s_ref):
    t = pl.program_id(1)
    tb = mix_ref.shape[1]

    @pl.when(t == 0)
    def _():
        ypad_ref[0:CV_HALO, :] = jnp.zeros((CV_HALO, CV_WIDTH), F32)
        dnpad_ref[0:DN_HALO, :] = jnp.zeros((DN_HALO, 3 * DN_WIDTH), F32)
        s_ref[...] = jnp.zeros(s_ref.shape, F32)

    row_i = lax.broadcasted_iota(jnp.int32, (GM_BLOCK, GM_BLOCK), 0)
    col_i = lax.broadcasted_iota(jnp.int32, (GM_BLOCK, GM_BLOCK), 1)
    causal = row_i >= col_i
    w_heads = [jnp.where(causal, ws_ref[h], 0.0).astype(BF16) for h in range(GM_HEADS)]
    head_of_lane = lax.broadcasted_iota(jnp.int32, (GM_BLOCK, GM_WIDTH), 1) // GM_HEAD_DIM
    gm_g = gm_g_ref[...]
    gm_b = gm_b_ref[...]
    for r in range(tb // GM_BLOCK):
        r0, r1 = r * GM_BLOCK, (r + 1) * GM_BLOCK
        u = _gelu(p_ref[0, r0:r1, OFF_GM_U:OFF_GM_U + GM_WIDTH])
        v = _gelu(p_ref[0, r0:r1, OFF_GM_V:OFF_GM_V + GM_WIDTH])
        vb = _layer_norm(v, gm_g, gm_b).astype(BF16)
        s = gm_bias_ref[...]
        for h in range(GM_HEADS):
            ph = jnp.dot(w_heads[h], vb, preferred_element_type=F32)
            s = s + jnp.where(head_of_lane == h, ph, 0.0)
        mix_ref[0, r0:r1, 0:GM_WIDTH] = (u * s).astype(BF16)

    a = p_ref[0, :, OFF_CV_A:OFF_CV_A + CV_WIDTH]
    g = p_ref[0, :, OFF_CV_G:OFF_CV_G + CV_WIDTH]
    ypad_ref[CV_HALO:CV_HALO + tb, :] = a * jax.nn.sigmoid(g)
    cvb = cvb_ref[...]
    cvg = cvg_ref[...]
    cvbeta = cvbeta_ref[...]
    lead = CV_HALO - (CV_KERNEL - 1)
    for rb in range(tb // CV_ROWS):
        base = rb * CV_ROWS + lead
        acc = jnp.broadcast_to(cvb, (CV_ROWS, CV_WIDTH))
        for k in range(CV_KERNEL):
            acc = acc + cvw_ref[k:k + 1, :] * ypad_ref[base + k:base + k + CV_ROWS, :]
        z = _layer_norm(acc, cvg, cvbeta)
        mix_ref[0, rb * CV_ROWS:(rb + 1) * CV_ROWS, GM_WIDTH:GM_WIDTH + CV_WIDTH] = (
            z * jax.nn.sigmoid(z)).astype(BF16)
    ypad_ref[0:CV_HALO, :] = ypad_ref[tb:tb + CV_HALO, :]

    dnpad_ref[DN_HALO:DN_HALO + tb, :] = p_ref[0, :, OFF_DN_QKV:OFF_DN_QKV + 3 * DN_WIDTH]
    lead = DN_HALO - (DN_CONV - 1)
    for c in range(3 * DN_HEADS):
        c0, c1 = c * DN_HEAD_DIM, (c + 1) * DN_HEAD_DIM
        for rb in range(tb // DN_ROWS):
            base = rb * DN_ROWS + lead
            acc = dnw_ref[0:1, c0:c1] * dnpad_ref[base:base + DN_ROWS, c0:c1]
            for k in range(1, DN_CONV):
                acc = acc + dnw_ref[k:k + 1, c0:c1] * dnpad_ref[base + k:base + k + DN_ROWS, c0:c1]
            z = acc * jax.nn.sigmoid(acc)
            if c < 2 * DN_HEADS:
                z = z * lax.rsqrt(jnp.sum(z * z, axis=-1, keepdims=True) + EPS)
                if c < DN_HEADS:
                    z = z * (DN_HEAD_DIM ** -0.5)
            qkv_ref[rb * DN_ROWS:(rb + 1) * DN_ROWS, c0:c1] = z
    dnpad_ref[0:DN_HALO, :] = dnpad_ref[tb:tb + DN_HALO, :]

    pba = p_ref[0, :, OFF_DN_BA:OFF_DN_BA + LANES]
    beta_ref[...] = jax.nn.sigmoid(pba)
    la_ref[...] = -jnp.exp(alog_ref[...]) * jax.nn.softplus(pba + dtb_ref[...])

    ri = lax.broadcasted_iota(jnp.int32, (CHUNK, CHUNK), 0)
    ci = lax.broadcasted_iota(jnp.int32, (CHUNK, CHUNK), 1)
    incl = ri >= ci
    strict = ri > ci
    tri = incl.astype(F32)
    same16 = (ri // 16) == (ci // 16)
    same32 = (ri // 32) == (ci // 32)
    m_bd16 = strict & same16
    m_e = strict & same32 & jnp.logical_not(same16)
    m_f = strict & jnp.logical_not(same32)
    dng = dng_ref[...]

    def chunk_body(c, carry):
        r0 = pl.multiple_of(c * CHUNK, CHUNK)
        rows = pl.ds(r0, CHUNK)
        la = la_ref[rows, :]
        beta = beta_ref[rows, :]
        gam = jnp.dot(tri, la, precision=lax.Precision.HIGHEST, preferred_element_type=F32)
        gl = gam[CHUNK - 1:CHUNK, :]
        egam = jnp.exp(gam)
        edec = jnp.exp(gl - gam)
        ecd = jnp.exp(gl)
        for h in range(DN_HEADS):
            q = qkv_ref[rows, h * DN_HEAD_DIM:(h + 1) * DN_HEAD_DIM]
            k = qkv_ref[rows, DN_WIDTH + h * DN_HEAD_DIM:DN_WIDTH + (h + 1) * DN_HEAD_DIM]
            v = qkv_ref[rows, 2 * DN_WIDTH + h * DN_HEAD_DIM:2 * DN_WIDTH + (h + 1) * DN_HEAD_DIM]
            gam_h = _lane_bcast(gam, DN_HEADS + h, CHUNK, DN_HEAD_DIM)
            egam_h = _lane_bcast(egam, DN_HEADS + h, CHUNK, DN_HEAD_DIM)
            edec_h = _lane_bcast(edec, DN_HEADS + h, CHUNK, DN_HEAD_DIM)
            beta_h = _lane_bcast(beta, h, CHUNK, DN_HEAD_DIM)
            cd_h = _lane_bcast(ecd, DN_HEADS + h, DN_HEAD_DIM, DN_HEAD_DIM)

            gam_row = gam_h.T[0:CHUNK, :]
            diff = gam_h[:, 0:CHUNK] - gam_row
            dec_incl = jnp.where(incl, jnp.exp(jnp.minimum(diff, 0.0)), 0.0)

            kb = k.astype(BF16)
            kk = _dot_nt(kb, kb)
            qk = _dot_nt(q, kb) * dec_incl
            m0 = -(beta_h[:, 0:CHUNK] * kk * dec_incl)

            a1 = jnp.where(m_bd16, m0, 0.0)
            a2 = _dot(a1, a1)
            a4 = _dot(a2, a2)
            a8 = _dot(a4, a4)
            x = a1 + a2 + _dot(a1, a2)
            x = x + a4 + _dot(x, a4)
            x = x + a8 + _dot(x, a8)
            e = jnp.where(m_e, m0, 0.0)
            y = e + _dot(x, e)
            x = x + y + _dot(y, x)
            f = jnp.where(m_f, m0, 0.0)
            y = f + _dot(x, f)
            x = x + y + _dot(y, x)

            rhs = jnp.concatenate([beta_h * v, (beta_h * egam_h) * k], axis=1)
            sol = rhs + _dot(x, rhs)
            u_part = sol[:, 0:DN_HEAD_DIM]
            w_part = sol[:, DN_HEAD_DIM:]

            s_old = s_ref[h]
            ws = _dot(jnp.concatenate([w_part, q * egam_h], axis=0), s_old)
            u = u_part - ws[0:CHUNK, :]
            o = ws[CHUNK:, :] + _dot(qk, u)
            s_ref[h] = cd_h * s_old + _dot_tn(k * edec_h, u)

            gate = p_ref[0, rows, OFF_DN_GATE + h * DN_HEAD_DIM:OFF_DN_GATE + (h + 1) * DN_HEAD_DIM]
            out = _rms_norm(o, dng) * (gate * jax.nn.sigmoid(gate))
            col0 = GM_WIDTH + CV_WIDTH + h * DN_HEAD_DIM
            mix_ref[0, rows, col0:col0 + DN_HEAD_DIM] = out.astype(BF16)
        return carry

    lax.fori_loop(0, tb // CHUNK, chunk_body, 0)


def _mix(p3d, prm):
    b, t, n_p = p3d.shape
    tb = TB_MIX
    d_mix = GM_WIDTH + CV_WIDTH + DN_WIDTH

    def full(a):
        nd = a.ndim
        return pl.BlockSpec(a.shape, lambda i, j, _nd=nd: (0,) * _nd)

    params = [prm["gm_ln_g"], prm["gm_ln_b"], prm["gm_ws"], prm["gm_bias"],
              prm["cv_w"], prm["cv_b"], prm["cv_ln_g"], prm["cv_ln_b"],
              prm["dn_w"], prm["dn_alog"], prm["dn_dtb"], prm["dn_norm_g"]]
    return pl.pallas_call(
        _mix_kernel,
        grid=(b, t // tb),
        in_specs=[pl.BlockSpec((1, tb, n_p), lambda i, j: (i, j, 0))] + [full(a) for a in params],
        out_specs=pl.BlockSpec((1, tb, d_mix), lambda i, j: (i, j, 0)),
        out_shape=jax.ShapeDtypeStruct((b, t, d_mix), BF16),
        scratch_shapes=[
            pltpu.VMEM((CV_HALO + tb, CV_WIDTH), F32),
            pltpu.VMEM((DN_HALO + tb, 3 * DN_WIDTH), F32),
            pltpu.VMEM((tb, 3 * DN_WIDTH), F32),
            pltpu.VMEM((tb, LANES), F32),
            pltpu.VMEM((tb, LANES), F32),
            pltpu.VMEM((DN_HEADS, DN_HEAD_DIM, DN_HEAD_DIM), F32),
        ],
        compiler_params=pltpu.CompilerParams(
            dimension_semantics=("arbitrary", "arbitrary"),
            vmem_limit_bytes=48 * 1024 * 1024),
    )(p3d, *params)


def _pad_rows(a, rows):
    return jnp.pad(a, ((0, rows - a.shape[0]), (0, 0)))


def kernel(x, mix_norm_g, w_in, gm_ln_g, gm_ln_b, gm_ws, gm_bs, cv_dw_w, cv_dw_b, cv_ln_g, cv_ln_b, dn_conv_w, dn_a_log, dn_dt_bias, dn_norm_g, w_out, ffn_norm_g, w_gate, w_up, w_down, final_norm_g):
    b, t, d = x.shape
    depth = w_in.shape[0]
    d_proj = w_in.shape[2]
    d_ff = w_gate.shape[2]
    nch = d_ff // FF_CHUNK
    assert t % TB_MIX == 0 and (b * t) % TM_IN == 0 and (b * t) % TM_FFN == 0
    assert d_ff % FF_CHUNK == 0 and d_proj == OFF_DN_BA + 2 * DN_HEADS

    x2d = x.reshape(b * t, d)
    for l in range(depth):
        w_pad = jnp.pad(w_in[l], ((0, 0), (0, D_PROJ_PAD - d_proj))).astype(BF16)
        p = _in_proj(x2d, mix_norm_g[l][None, :], w_pad)

        lane_pad = (0, LANES - 2 * DN_HEADS)
        prm = {
            "gm_ln_g": gm_ln_g[l][None, :], "gm_ln_b": gm_ln_b[l][None, :],
            "gm_ws": gm_ws[l],
            "gm_bias": jnp.repeat(gm_bs[l].T, GM_HEAD_DIM, axis=1),
            "cv_w": _pad_rows(cv_dw_w[l], 32), "cv_b": cv_dw_b[l][None, :],
            "cv_ln_g": cv_ln_g[l][None, :], "cv_ln_b": cv_ln_b[l][None, :],
            "dn_w": _pad_rows(dn_conv_w[l], 8),
            "dn_alog": jnp.pad(jnp.concatenate([jnp.zeros((DN_HEADS,), F32), dn_a_log[l]]), lane_pad)[None, :],
            "dn_dtb": jnp.pad(jnp.concatenate([jnp.zeros((DN_HEADS,), F32), dn_dt_bias[l]]), lane_pad)[None, :],
            "dn_norm_g": dn_norm_g[l][None, :],
        }
        mix = _mix(p.reshape(b, t, D_PROJ_PAD), prm)

        wgu = jnp.concatenate(
            [w_gate[l].reshape(d, nch, FF_CHUNK), w_up[l].reshape(d, nch, FF_CHUNK)], axis=2)
        wgu = wgu.transpose(1, 0, 2).astype(BF16)
        wd = w_down[l].reshape(nch, FF_CHUNK, d).astype(BF16)
        x2d = _out_ffn(x2d, mix.reshape(b * t, -1), w_out[l].astype(BF16), ffn_norm_g[l][None, :],
                       wgu, wd, final_norm_g[None, :], final_norm=(l == depth - 1))
    return x2d.reshape(b, t, d)
```

```python
import functools

import jax
import jax.numpy as jnp
from jax import lax
from jax.experimental import pallas as pl
from jax.experimental.pallas import tpu as pltpu

F32 = jnp.float32
BF16 = jnp.bfloat16
EPS = 1e-6

LANES = 128
V7X_VMEM_BYTES = 64 * 1024 * 1024

GM_WIDTH = 256
GM_HEADS = 4
GM_HEAD_DIM = GM_WIDTH // GM_HEADS
GM_BLOCK = 128
CV_WIDTH = 256
CV_KERNEL = 31
DN_WIDTH = 512
DN_HEAD_DIM = 128
DN_HEADS = 4
DN_CONV = 4
CHUNK = 64

OFF_GM_U = 0
OFF_GM_V = OFF_GM_U + GM_WIDTH
OFF_CV_A = OFF_GM_V + GM_WIDTH
OFF_CV_G = OFF_CV_A + CV_WIDTH
OFF_DN_QKV = OFF_CV_G + CV_WIDTH
OFF_DN_GATE = OFF_DN_QKV + 3 * DN_WIDTH
OFF_DN_BA = OFF_DN_GATE + DN_WIDTH
D_PROJ_PAD = OFF_DN_BA + LANES

TM_IN = 512
TM_FFN = 512
TB_MIX = 128
NB_MIX = 4
FF_CHUNK = 256
CV_HALO = 32
DN_HALO = 8
CV_TAIL = 8


def _rms_norm(x, g):
    ms = jnp.mean(x * x, axis=-1, keepdims=True)
    return x * lax.rsqrt(ms + EPS) * g


def _layer_norm(x, g, b):
    mu = jnp.mean(x, axis=-1, keepdims=True)
    xc = x - mu
    var = jnp.mean(xc * xc, axis=-1, keepdims=True)
    return xc * lax.rsqrt(var + EPS) * g + b


def _gelu(x):
    return 0.5 * x * (1.0 + lax.erf(x * (2.0 ** -0.5)))


def _dot_nt(a, b):
    return lax.dot_general(a.astype(BF16), b.astype(BF16), (((1,), (1,)), ((), ())),
                           preferred_element_type=F32)


def _dot_tn(a, b):
    return lax.dot_general(a.astype(BF16), b.astype(BF16), (((0,), (0,)), ((), ())),
                           preferred_element_type=F32)


def _in_proj_kernel(x_ref, g_ref, w_ref, o_ref):
    h = _rms_norm(x_ref[...], g_ref[...]).astype(BF16)
    n = w_ref.shape[1]
    step = 512
    for c0 in range(0, n, step):
        c1 = min(c0 + step, n)
        o_ref[:, c0:c1] = jnp.dot(h, w_ref[:, c0:c1], preferred_element_type=F32)


def _in_proj(x2d, g, w_pad):
    n_tok, d = x2d.shape
    n_out = w_pad.shape[1]
    return pl.pallas_call(
        _in_proj_kernel,
        grid=(n_tok // TM_IN,),
        in_specs=[
            pl.BlockSpec((TM_IN, d), lambda i: (i, 0)),
            pl.BlockSpec((1, d), lambda i: (0, 0)),
            pl.BlockSpec((d, n_out), lambda i: (0, 0)),
        ],
        out_specs=pl.BlockSpec((TM_IN, n_out), lambda i: (i, 0)),
        out_shape=jax.ShapeDtypeStruct((n_tok, n_out), F32),
        compiler_params=pltpu.CompilerParams(
            dimension_semantics=("arbitrary",),
            vmem_limit_bytes=48 * 1024 * 1024),
    )(x2d, g, w_pad)


def _out_ffn_kernel(x_ref, mix_ref, wo_ref, g_ref, wgu_ref, wd_ref, gf_ref, o_ref,
                    acc_ref, h_ref, *, final_norm):
    xn = x_ref[...] + jnp.dot(mix_ref[...], wo_ref[...], preferred_element_type=F32)
    acc_ref[...] = xn
    h_ref[...] = _rms_norm(xn, g_ref[...]).astype(BF16)

    def body(j, carry):
        gu = jnp.dot(h_ref[...], wgu_ref[j], preferred_element_type=F32)
        gate = gu[:, :FF_CHUNK]
        up = gu[:, FF_CHUNK:]
        act = (gate * jax.nn.sigmoid(gate) * up).astype(BF16)
        acc_ref[...] += jnp.dot(act, wd_ref[j], preferred_element_type=F32)
        return carry

    lax.fori_loop(0, wgu_ref.shape[0], body, 0)
    y = acc_ref[...]
    if final_norm:
        y = _rms_norm(y, gf_ref[...])
    o_ref[...] = y


def _out_ffn(x2d, mix2d, wo, g, wgu, wd, gf, final_norm):
    n_tok, d = x2d.shape
    const2 = lambda i: (0, 0)
    const3 = lambda i: (0, 0, 0)
    return pl.pallas_call(
        functools.partial(_out_ffn_kernel, final_norm=final_norm),
        grid=(n_tok // TM_FFN,),
        in_specs=[
            pl.BlockSpec((TM_FFN, d), lambda i: (i, 0)),
            pl.BlockSpec((TM_FFN, mix2d.shape[1]), lambda i: (i, 0)),
            pl.BlockSpec(wo.shape, const2),
            pl.BlockSpec((1, d), const2),
            pl.BlockSpec(wgu.shape, const3),
            pl.BlockSpec(wd.shape, const3),
            pl.BlockSpec((1, d), const2),
        ],
        out_specs=pl.BlockSpec((TM_FFN, d), lambda i: (i, 0)),
        out_shape=jax.ShapeDtypeStruct((n_tok, d), F32),
        scratch_shapes=[pltpu.VMEM((TM_FFN, d), F32), pltpu.VMEM((TM_FFN, d), BF16)],
        compiler_params=pltpu.CompilerParams(
            dimension_semantics=("arbitrary",),
            vmem_limit_bytes=56 * 1024 * 1024),
    )(x2d, mix2d, wo, g, wgu, wd, gf)


def _bf(x):
    return x.astype(BF16)


def _cat_rows(xs):
    return jnp.concatenate(xs, axis=0)


def _cat_lanes(xs):
    return jnp.concatenate(xs, axis=1)


def _mix_kernel(p_ref, gm_g_ref, gm_b_ref, wcat_ref, gm_bias_ref,
                cvw_ref, cvb_ref, cvg_ref, cvbeta_ref,
                dnw_ref, alog_ref, dtb_ref, dng_ref,
                mix_ref,
                ypad_ref, yhalo_ref, dnpad_ref, dnhalo_ref, qkv_ref, gam_ref, betar_ref, s_ref):
    t = pl.program_id(1)
    nb, tb = mix_ref.shape[0], mix_ref.shape[1]
    n_rows = nb * tb
    cpb = tb // CHUNK
    hd = DN_HEAD_DIM
    pw = 2 * hd

    @pl.when(t == 0)
    def _():
        yhalo_ref[...] = jnp.zeros(yhalo_ref.shape, F32)
        dnhalo_ref[...] = jnp.zeros(dnhalo_ref.shape, F32)
        s_ref[...] = jnp.zeros(s_ref.shape, F32)
        ypad_ref[CV_HALO + tb:, :] = jnp.zeros((CV_TAIL, CV_WIDTH), F32)

    row_i = lax.broadcasted_iota(jnp.int32, (GM_BLOCK, GM_HEADS * GM_BLOCK), 0)
    col_i = lax.broadcasted_iota(jnp.int32, (GM_BLOCK, GM_HEADS * GM_BLOCK), 1) % GM_BLOCK
    w_cat = _bf(jnp.where(row_i >= col_i, wcat_ref[...], 0.0))
    head_of_lane = lax.broadcasted_iota(jnp.int32, (GM_BLOCK, GM_WIDTH), 1) // GM_HEAD_DIM
    gm_g = gm_g_ref[...]
    gm_b = gm_b_ref[...]
    for b in range(nb):
        u = _gelu(p_ref[b, :, OFF_GM_U:OFF_GM_U + GM_WIDTH])
        v = _gelu(p_ref[b, :, OFF_GM_V:OFF_GM_V + GM_WIDTH])
        vn = _layer_norm(v, gm_g, gm_b)
        vbd = _cat_rows([_bf(jnp.where(head_of_lane == h, vn, 0.0)) for h in range(GM_HEADS)])
        s = gm_bias_ref[...] + jnp.dot(w_cat, vbd, preferred_element_type=F32)
        mix_ref[b, :, 0:GM_WIDTH] = _bf(u * s)

    cvb = cvb_ref[...]
    cvg = cvg_ref[...]
    cvbeta = cvbeta_ref[...]
    lead = CV_HALO - (CV_KERNEL - 1)
    slab = tb + 8
    for b in range(nb):
        ypad_ref[0:CV_HALO, :] = yhalo_ref[b]
        a = p_ref[b, :, OFF_CV_A:OFF_CV_A + CV_WIDTH]
        g = p_ref[b, :, OFF_CV_G:OFF_CV_G + CV_WIDTH]
        ypad_ref[CV_HALO:CV_HALO + tb, :] = a * jax.nn.sigmoid(g)
        acc = jnp.broadcast_to(cvb, (tb, CV_WIDTH))
        for r in range(8):
            z = None
            for s_off in range(lead, lead + CV_KERNEL):
                if s_off % 8 != r:
                    continue
                k = s_off - lead
                a8 = (s_off // 8) * 8
                term = cvw_ref[k:k + 1, :] * ypad_ref[a8:a8 + slab, :]
                z = term if z is None else z + term
            if z is not None:
                acc = acc + z[r:r + tb, :]
        yhalo_ref[b] = ypad_ref[tb:tb + CV_HALO, :]
        zn = _layer_norm(acc, cvg, cvbeta)
        mix_ref[b, :, GM_WIDTH:GM_WIDTH + CV_WIDTH] = _bf(zn * jax.nn.sigmoid(zn))

    lead = DN_HALO - (DN_CONV - 1)
    for b in range(nb):
        dnpad_ref[0:DN_HALO, :] = dnhalo_ref[b]
        dnpad_ref[DN_HALO:DN_HALO + tb, :] = p_ref[b, :, OFF_DN_QKV:OFF_DN_QKV + 3 * DN_WIDTH]
        for c in range(3 * DN_HEADS):
            c0, c1 = c * hd, (c + 1) * hd
            acc = dnw_ref[0:1, c0:c1] * dnpad_ref[lead:lead + tb, c0:c1]
            for k in range(1, DN_CONV):
                acc = acc + dnw_ref[k:k + 1, c0:c1] * dnpad_ref[lead + k:lead + k + tb, c0:c1]
            z = acc * jax.nn.sigmoid(acc)
            if c < 2 * DN_HEADS:
                z = z * lax.rsqrt(jnp.sum(z * z, axis=-1, keepdims=True) + EPS)
                if c < DN_HEADS:
                    z = z * (hd ** -0.5)
            qkv_ref[b * tb:(b + 1) * tb, c0:c1] = z
        dnhalo_ref[b] = dnpad_ref[tb:tb + DN_HALO, :]

    pba = p_ref[:, :, OFF_DN_BA:OFF_DN_BA + LANES].reshape(n_rows, LANES)
    beta_c = jax.nn.sigmoid(pba)
    gam_c = -jnp.exp(alog_ref[...]) * jax.nn.softplus(pba + dtb_ref[...])
    row_in_chunk = lax.broadcasted_iota(jnp.int32, (n_rows, LANES), 0) % CHUNK
    shift = 1
    while shift < CHUNK:
        gam_c = gam_c + jnp.where(row_in_chunk >= shift, pltpu.roll(gam_c, shift, 0), 0.0)
        shift *= 2
    for h in range(DN_HEADS):
        gam_ref[:, h * hd:(h + 1) * hd] = jnp.broadcast_to(
            gam_c[:, DN_HEADS + h:DN_HEADS + h + 1], (n_rows, hd))
        betar_ref[:, h * hd:(h + 1) * hd] = jnp.broadcast_to(beta_c[:, h:h + 1], (n_rows, hd))

    ri = lax.broadcasted_iota(jnp.int32, (CHUNK, LANES), 0)
    lane = lax.broadcasted_iota(jnp.int32, (CHUNK, LANES), 1)
    ci = lane % CHUNK
    first = lane < CHUNK
    incl = ri >= ci
    strict = ri > ci
    same16 = (ri // 16) == (ci // 16)
    same32 = (ri // 32) == (ci // 32)
    m_bd16 = strict & same16
    m_e = strict & same32 & jnp.logical_not(same16)
    m_f = strict & jnp.logical_not(same32)
    zeros_h = jnp.zeros((CHUNK, hd), BF16)

    def pdot(x, y):
        ybd = _cat_rows([_bf(jnp.where(first, y, 0.0)), _bf(jnp.where(first, 0.0, y))])
        return jnp.dot(_bf(x), ybd, preferred_element_type=F32)

    chains = [(g, p) for g in range(nb * cpb) for p in range(DN_HEADS // 2)]
    lane2 = lax.broadcasted_iota(jnp.int32, (CHUNK, pw), 1)
    head0 = lane2 < hd

    def per_chain(fn, *cols):
        return [fn(*args) for args in zip(*cols)]

    def load(ref, col_off):
        return [ref[g * CHUNK:(g + 1) * CHUNK, col_off + p * pw:col_off + (p + 1) * pw] for g, p in chains]

    gr = load(gam_ref, 0)
    br = load(betar_ref, 0)
    q = load(qkv_ref, 0)
    k = load(qkv_ref, DN_WIDTH)

    def decay(gr_c):
        gcol = jnp.where(first, gr_c[:, 0:hd], gr_c[:, hd:pw])
        grow = _cat_rows([gr_c[:, 0:hd], gr_c[:, hd:pw]]).T[0:CHUNK, :]
        return jnp.where(incl, jnp.exp(jnp.minimum(gcol - grow, 0.0)), 0.0)

    dec = per_chain(decay, gr)
    qkkk = per_chain(
        lambda q_c, k_c: _dot_nt(
            _cat_rows([_bf(q_c), _bf(k_c)]),
            _cat_rows([_bf(jnp.where(head0, k_c, 0.0)), _bf(jnp.where(head0, 0.0, k_c))])),
        q, k)
    qkm = per_chain(lambda m, d: _bf(m[0:CHUNK, :] * d), qkkk, dec)
    m0 = per_chain(
        lambda m, d, br_c: -(jnp.where(first, br_c[:, 0:hd], br_c[:, hd:pw]) * m[CHUNK:, :] * d),
        qkkk, dec, br)

    a1 = per_chain(lambda m: jnp.where(m_bd16, m, 0.0), m0)
    a2 = per_chain(pdot, a1, a1)
    a4 = per_chain(pdot, a2, a2)
    a8 = per_chain(pdot, a4, a4)
    x = per_chain(lambda a, b_, c: a + b_ + c, a1, a2, per_chain(pdot, a1, a2))
    x = per_chain(lambda a, b_, c: a + b_ + c, x, a4, per_chain(pdot, x, a4))
    x = per_chain(lambda a, b_, c: a + b_ + c, x, a8, per_chain(pdot, x, a8))
    for mask in (m_e, m_f):
        e = per_chain(lambda m: jnp.where(mask, m, 0.0), m0)
        y = per_chain(lambda a, b_: a + b_, e, per_chain(pdot, x, e))
        x = per_chain(lambda a, b_, c: a + b_ + c, x, y, per_chain(pdot, y, x))

    v = load(qkv_ref, 2 * DN_WIDTH)
    egam = per_chain(jnp.exp, gr)
    bv = per_chain(lambda a, b_: a * b_, br, v)
    bk = per_chain(lambda a, e_, k_c: a * e_ * k_c, br, egam, k)

    def solve(x_c, bv_c, bk_c):
        rhs = _cat_lanes([bv_c[:, 0:hd], bk_c[:, 0:hd], bv_c[:, hd:pw], bk_c[:, hd:pw]])
        rhs_bd = _cat_rows([
            _cat_lanes([_bf(bv_c[:, 0:hd]), _bf(bk_c[:, 0:hd]), zeros_h, zeros_h]),
            _cat_lanes([zeros_h, zeros_h, _bf(bv_c[:, hd:pw]), _bf(bk_c[:, hd:pw])])])
        return rhs + jnp.dot(_bf(x_c), rhs_bd, preferred_element_type=F32)

    sol = per_chain(solve, x, bv, bk)
    u_part = per_chain(lambda s_: _cat_lanes([s_[:, 0:hd], s_[:, 2 * hd:3 * hd]]), sol)
    w_part = per_chain(lambda s_: _bf(_cat_lanes([s_[:, hd:2 * hd], s_[:, 3 * hd:4 * hd]])), sol)
    qd = per_chain(lambda q_c, e_: _bf(q_c * e_), q, egam)
    kd = per_chain(lambda k_c, g_c: _bf(k_c * jnp.exp(g_c[CHUNK - 1:CHUNK, :] - g_c)), k, gr)
    cd = per_chain(lambda g_c: jnp.exp(g_c[CHUNK - 1:CHUNK, :]), gr)

    dng = dng_ref[...]
    zeros_s = jnp.zeros((hd, hd), BF16)
    state = [[s_ref[b * DN_HEADS + h] for h in range(DN_HEADS)] for b in range(nb)]
    out_rows = [[None] * cpb for _ in range(nb)]
    pairs = [(b, p) for b in range(nb) for p in range(DN_HEADS // 2)]
    for j in range(cpb):
        idx = [chains.index((b * cpb + j, p)) for b, p in pairs]
        sbd = [_cat_rows([_cat_lanes([_bf(state[b][2 * p]), zeros_s]),
                          _cat_lanes([zeros_s, _bf(state[b][2 * p + 1])])]) for b, p in pairs]
        ws = [jnp.dot(_cat_rows([w_part[i], qd[i]]), s_, preferred_element_type=F32)
              for i, s_ in zip(idx, sbd)]
        ub = [_bf(u_part[i] - w_[0:CHUNK, :]) for i, w_ in zip(idx, ws)]
        ubd = [_cat_rows([_cat_lanes([u_[:, 0:hd], zeros_h]), _cat_lanes([zeros_h, u_[:, hd:pw]])]) for u_ in ub]
        o = [w_[CHUNK:, :] + jnp.dot(qkm[i], u_, preferred_element_type=F32)
             for i, w_, u_ in zip(idx, ws, ubd)]
        ktu = [[_dot_tn(kd[i][:, hh * hd:(hh + 1) * hd], u_[:, hh * hd:(hh + 1) * hd]) for hh in range(2)]
               for i, u_ in zip(idx, ub)]
        for n, (b, p) in enumerate(pairs):
            for hh in range(2):
                c0, c1 = hh * hd, (hh + 1) * hd
                state[b][2 * p + hh] = state[b][2 * p + hh] * cd[idx[n]][:, c0:c1] + ktu[n][hh]
        for b in range(nb):
            gate = p_ref[b, j * CHUNK:(j + 1) * CHUNK, OFF_DN_GATE:OFF_DN_GATE + DN_WIDTH]
            o_b = _cat_lanes([o[pairs.index((b, p))] for p in range(DN_HEADS // 2)])
            normed = _cat_lanes([_rms_norm(o_b[:, h * hd:(h + 1) * hd], dng) for h in range(DN_HEADS)])
            out_rows[b][j] = _bf(normed * (gate * jax.nn.sigmoid(gate)))
    mix_ref[:, :, GM_WIDTH + CV_WIDTH:] = jnp.stack([_cat_rows(out_rows[b]) for b in range(nb)], axis=0)
    s_ref[...] = jnp.stack([state[b][h] for b in range(nb) for h in range(DN_HEADS)], axis=0)


def _mix(p3d, prm):
    b, t, n_p = p3d.shape
    nb, tb = NB_MIX, TB_MIX
    d_mix = GM_WIDTH + CV_WIDTH + DN_WIDTH
    n_rows = nb * tb
    n_chunks = n_rows // CHUNK

    def full(a):
        nd = a.ndim
        return pl.BlockSpec(a.shape, lambda i, j, _nd=nd: (0,) * _nd)

    params = [prm["gm_ln_g"], prm["gm_ln_b"], prm["gm_wcat"], prm["gm_bias"],
              prm["cv_w"], prm["cv_b"], prm["cv_ln_g"], prm["cv_ln_b"],
              prm["dn_w"], prm["dn_alog"], prm["dn_dtb"], prm["dn_norm_g"]]
    return pl.pallas_call(
        _mix_kernel,
        grid=(b // nb, t // tb),
        in_specs=[pl.BlockSpec((nb, tb, n_p), lambda i, j: (i, j, 0))] + [full(a) for a in params],
        out_specs=pl.BlockSpec((nb, tb, d_mix), lambda i, j: (i, j, 0)),
        out_shape=jax.ShapeDtypeStruct((b, t, d_mix), BF16),
        scratch_shapes=[
            pltpu.VMEM((CV_HALO + tb + CV_TAIL, CV_WIDTH), F32),
            pltpu.VMEM((nb, CV_HALO, CV_WIDTH), F32),
            pltpu.VMEM((DN_HALO + tb, 3 * DN_WIDTH), F32),
            pltpu.VMEM((nb, DN_HALO, 3 * DN_WIDTH), F32),
            pltpu.VMEM((n_rows, 3 * DN_WIDTH), F32),
            pltpu.VMEM((n_rows, DN_WIDTH), F32),
            pltpu.VMEM((n_rows, DN_WIDTH), F32),
            pltpu.VMEM((nb * DN_HEADS, DN_HEAD_DIM, DN_HEAD_DIM), F32),
        ],
        compiler_params=pltpu.CompilerParams(
            dimension_semantics=("arbitrary", "arbitrary"),
            vmem_limit_bytes=48 * 1024 * 1024),
    )(p3d, *params)


def _pad_rows(a, rows):
    return jnp.pad(a, ((0, rows - a.shape[0]), (0, 0)))


def kernel(x, mix_norm_g, w_in, gm_ln_g, gm_ln_b, gm_ws, gm_bs, cv_dw_w, cv_dw_b, cv_ln_g, cv_ln_b, dn_conv_w, dn_a_log, dn_dt_bias, dn_norm_g, w_out, ffn_norm_g, w_gate, w_up, w_down, final_norm_g):
    b, t, d = x.shape
    depth = w_in.shape[0]
    d_proj = w_in.shape[2]
    d_ff = w_gate.shape[2]
    nch = d_ff // FF_CHUNK
    assert t % TB_MIX == 0 and b % NB_MIX == 0 and TB_MIX == GM_BLOCK
    assert (b * t) % TM_IN == 0 and (b * t) % TM_FFN == 0
    assert d_ff % FF_CHUNK == 0 and d_proj == OFF_DN_BA + 2 * DN_HEADS

    x2d = x.reshape(b * t, d)
    for l in range(depth):
        w_pad = jnp.pad(w_in[l], ((0, 0), (0, D_PROJ_PAD - d_proj))).astype(BF16)
        p = _in_proj(x2d, mix_norm_g[l][None, :], w_pad)

        lane_pad = (0, LANES - 2 * DN_HEADS)
        prm = {
            "gm_ln_g": gm_ln_g[l][None, :], "gm_ln_b": gm_ln_b[l][None, :],
            "gm_wcat": gm_ws[l].transpose(1, 0, 2).reshape(GM_BLOCK, GM_HEADS * GM_BLOCK),
            "gm_bias": jnp.repeat(gm_bs[l].T, GM_HEAD_DIM, axis=1),
            "cv_w": _pad_rows(cv_dw_w[l], 32), "cv_b": cv_dw_b[l][None, :],
            "cv_ln_g": cv_ln_g[l][None, :], "cv_ln_b": cv_ln_b[l][None, :],
            "dn_w": _pad_rows(dn_conv_w[l], 8),
            "dn_alog": jnp.pad(jnp.concatenate([jnp.zeros((DN_HEADS,), F32), dn_a_log[l]]), lane_pad)[None, :],
            "dn_dtb": jnp.pad(jnp.concatenate([jnp.zeros((DN_HEADS,), F32), dn_dt_bias[l]]), lane_pad)[None, :],
            "dn_norm_g": dn_norm_g[l][None, :],
        }
        mix = _mix(p.reshape(b, t, D_PROJ_PAD), prm)

        wgu = jnp.concatenate(
            [w_gate[l].reshape(d, nch, FF_CHUNK), w_up[l].reshape(d, nch, FF_CHUNK)], axis=2)
        wgu = wgu.transpose(1, 0, 2).astype(BF16)
        wd = w_down[l].reshape(nch, FF_CHUNK, d).astype(BF16)
        x2d = _out_ffn(x2d, mix.reshape(b * t, -1), w_out[l].astype(BF16), ffn_norm_g[l][None, :],
                       wgu, wd, final_norm_g[None, :], final_norm=(l == depth - 1))
    return x2d.reshape(b, t, d)
```

```python
import functools

import jax
import jax.numpy as jnp
from jax import lax
from jax.experimental import pallas as pl
from jax.experimental.pallas import tpu as pltpu

F32 = jnp.float32
BF16 = jnp.bfloat16
EPS = 1e-6

LANES = 128
V7X_VMEM_BYTES = 64 * 1024 * 1024

GM_WIDTH = 256
GM_HEADS = 4
GM_HEAD_DIM = GM_WIDTH // GM_HEADS
GM_BLOCK = 128
CV_WIDTH = 256
CV_KERNEL = 31
DN_WIDTH = 512
DN_HEAD_DIM = 128
DN_HEADS = 4
DN_CONV = 4
CHUNK = 64

OFF_GM_U = 0
OFF_GM_V = OFF_GM_U + GM_WIDTH
OFF_CV_A = OFF_GM_V + GM_WIDTH
OFF_CV_G = OFF_CV_A + CV_WIDTH
OFF_DN_QKV = OFF_CV_G + CV_WIDTH
OFF_DN_GATE = OFF_DN_QKV + 3 * DN_WIDTH
OFF_DN_BA = OFF_DN_GATE + DN_WIDTH
D_PROJ_PAD = OFF_DN_BA + LANES

TM_FFN = 512
TB_MIX = 128
NB_MIX = 4
FF_CHUNK = 256
CV_HALO = 32
DN_HALO = 8
CV_TAIL = 8


def _rms_norm(x, g):
    ms = jnp.mean(x * x, axis=-1, keepdims=True)
    return x * lax.rsqrt(ms + EPS) * g


def _layer_norm(x, g, b):
    mu = jnp.mean(x, axis=-1, keepdims=True)
    xc = x - mu
    var = jnp.mean(xc * xc, axis=-1, keepdims=True)
    return xc * lax.rsqrt(var + EPS) * g + b


def _gelu(x):
    return 0.5 * x * (1.0 + lax.erf(x * (2.0 ** -0.5)))


def _dot_nt(a, b):
    return lax.dot_general(a.astype(BF16), b.astype(BF16), (((1,), (1,)), ((), ())),
                           preferred_element_type=F32)


def _dot_tn(a, b):
    return lax.dot_general(a.astype(BF16), b.astype(BF16), (((0,), (0,)), ((), ())),
                           preferred_element_type=F32)


def _ffn_kernel(x_ref, g_ref, wg_ref, wu_ref, wd_ref, gf_ref, o_ref, *, final_norm):
    x = x_ref[...]
    h = _rms_norm(x, g_ref[...]).astype(BF16)
    acc = x
    d_ff = wg_ref.shape[1]
    for c0 in range(0, d_ff, FF_CHUNK):
        c1 = c0 + FF_CHUNK
        gate = jnp.dot(h, wg_ref[:, c0:c1], preferred_element_type=F32)
        up = jnp.dot(h, wu_ref[:, c0:c1], preferred_element_type=F32)
        act = (gate * jax.nn.sigmoid(gate) * up).astype(BF16)
        acc = acc + jnp.dot(act, wd_ref[c0:c1, :], preferred_element_type=F32)
    if final_norm:
        acc = _rms_norm(acc, gf_ref[...])
    o_ref[...] = acc


def _ffn(x2d, g, wg, wu, wd, gf, final_norm):
    n_tok, d = x2d.shape
    const2 = lambda i: (0, 0)
    return pl.pallas_call(
        functools.partial(_ffn_kernel, final_norm=final_norm),
        grid=(n_tok // TM_FFN,),
        in_specs=[
            pl.BlockSpec((TM_FFN, d), lambda i: (i, 0)),
            pl.BlockSpec((1, d), const2),
            pl.BlockSpec(wg.shape, const2),
            pl.BlockSpec(wu.shape, const2),
            pl.BlockSpec(wd.shape, const2),
            pl.BlockSpec((1, d), const2),
        ],
        out_specs=pl.BlockSpec((TM_FFN, d), lambda i: (i, 0)),
        out_shape=jax.ShapeDtypeStruct((n_tok, d), F32),
        compiler_params=pltpu.CompilerParams(
            dimension_semantics=("arbitrary",),
            vmem_limit_bytes=56 * 1024 * 1024),
    )(x2d, g, wg, wu, wd, gf)


def _bf(x):
    return x.astype(BF16)


def _cat_rows(xs):
    return jnp.concatenate(xs, axis=0)


def _cat_lanes(xs):
    return jnp.concatenate(xs, axis=1)


def _mix_kernel(x_ref, g_in_ref, win_ref, wout_ref,
                gm_g_ref, gm_b_ref, wcat_ref, gm_bias_ref,
                cvw_ref, cvb_ref, cvg_ref, cvbeta_ref,
                dnw_ref, alog_ref, dtb_ref, dng_ref,
                out_ref,
                h_ref, pg_ref, pc_ref, ypad_ref, yhalo_ref, dnpad_ref, dnhalo_ref, qkv_ref, gate_ref, pba_ref,
                gam_ref, betar_ref, s_ref):
    t = pl.program_id(1)
    nb, tb = out_ref.shape[0], out_ref.shape[1]
    n_rows = nb * tb
    cpb = tb // CHUNK
    hd = DN_HEAD_DIM
    pw = 2 * hd

    @pl.when(t == 0)
    def _():
        yhalo_ref[...] = jnp.zeros(yhalo_ref.shape, F32)
        dnhalo_ref[...] = jnp.zeros(dnhalo_ref.shape, F32)
        s_ref[...] = jnp.zeros(s_ref.shape, F32)
        ypad_ref[CV_HALO + tb:, :] = jnp.zeros((CV_TAIL, CV_WIDTH), F32)

    row_i = lax.broadcasted_iota(jnp.int32, (GM_BLOCK, GM_HEADS * GM_BLOCK), 0)
    col_i = lax.broadcasted_iota(jnp.int32, (GM_BLOCK, GM_HEADS * GM_BLOCK), 1) % GM_BLOCK
    w_cat = _bf(jnp.where(row_i >= col_i, wcat_ref[...], 0.0))
    head_of_lane = lax.broadcasted_iota(jnp.int32, (GM_BLOCK, GM_WIDTH), 1) // GM_HEAD_DIM
    gm_g = gm_g_ref[...]
    gm_b = gm_b_ref[...]
    cvb = cvb_ref[...]
    cvg = cvg_ref[...]
    cvbeta = cvbeta_ref[...]
    cv_lead = CV_HALO - (CV_KERNEL - 1)
    dn_lead = DN_HALO - (DN_CONV - 1)
    slab = tb + 8
    g_in = g_in_ref[...]

    def proj(hb, c0, c1):
        return jnp.dot(hb, win_ref[:, c0:c1], preferred_element_type=F32)

    for b in range(nb):
        h_ref[b] = _bf(_rms_norm(x_ref[b], g_in))
        pg_ref[b] = proj(h_ref[b], OFF_GM_U, OFF_CV_A)

    ya = []
    for b in range(nb):
        pc_ref[b] = proj(h_ref[b], OFF_CV_A, OFF_DN_QKV)
        u = _gelu(pg_ref[b, :, 0:GM_WIDTH])
        vn = _layer_norm(_gelu(pg_ref[b, :, GM_WIDTH:]), gm_g, gm_b)
        vbd = _cat_rows([_bf(jnp.where(head_of_lane == h, vn, 0.0)) for h in range(GM_HEADS)])
        s = gm_bias_ref[...] + jnp.dot(w_cat, vbd, preferred_element_type=F32)
        ya.append(_bf(u * s))

    for b in range(nb):
        dnpad_ref[b, DN_HALO:DN_HALO + tb, :] = proj(h_ref[b], OFF_DN_QKV, OFF_DN_GATE)
        acc_x = x_ref[b] + jnp.dot(ya[b], wout_ref[0:GM_WIDTH, :], preferred_element_type=F32)
        ypad_ref[0:CV_HALO, :] = yhalo_ref[b]
        ypad_ref[CV_HALO:CV_HALO + tb, :] = (
            pc_ref[b, :, 0:CV_WIDTH] * jax.nn.sigmoid(pc_ref[b, :, CV_WIDTH:]))
        acc = jnp.broadcast_to(cvb, (tb, CV_WIDTH))
        for r in range(8):
            z = None
            for s_off in range(cv_lead, cv_lead + CV_KERNEL):
                if s_off % 8 != r:
                    continue
                k = s_off - cv_lead
                a8 = (s_off // 8) * 8
                term = cvw_ref[k:k + 1, :] * ypad_ref[a8:a8 + slab, :]
                z = term if z is None else z + term
            if z is not None:
                acc = acc + z[r:r + tb, :]
        yhalo_ref[b] = ypad_ref[tb:tb + CV_HALO, :]
        zn = _layer_norm(acc, cvg, cvbeta)
        out_ref[b] = acc_x + jnp.dot(_bf(zn * jax.nn.sigmoid(zn)),
                                     wout_ref[GM_WIDTH:GM_WIDTH + CV_WIDTH, :], preferred_element_type=F32)

    for b in range(nb):
        pgb = proj(h_ref[b], OFF_DN_GATE, D_PROJ_PAD)
        gate_ref[b * tb:(b + 1) * tb, :] = pgb[:, 0:DN_WIDTH]
        pba_ref[b * tb:(b + 1) * tb, :] = pgb[:, DN_WIDTH:]
        dnpad_ref[b, 0:DN_HALO, :] = dnhalo_ref[b]
        for c in range(3 * DN_HEADS):
            c0, c1 = c * hd, (c + 1) * hd
            acc = dnw_ref[0:1, c0:c1] * dnpad_ref[b, dn_lead:dn_lead + tb, c0:c1]
            for k in range(1, DN_CONV):
                acc = acc + dnw_ref[k:k + 1, c0:c1] * dnpad_ref[b, dn_lead + k:dn_lead + k + tb, c0:c1]
            z = acc * jax.nn.sigmoid(acc)
            if c < 2 * DN_HEADS:
                z = z * lax.rsqrt(jnp.sum(z * z, axis=-1, keepdims=True) + EPS)
                if c < DN_HEADS:
                    z = z * (hd ** -0.5)
            qkv_ref[b * tb:(b + 1) * tb, c0:c1] = z
        dnhalo_ref[b] = dnpad_ref[b, tb:tb + DN_HALO, :]

    pba = pba_ref[...]
    beta_c = jax.nn.sigmoid(pba)
    gam_c = -jnp.exp(alog_ref[...]) * jax.nn.softplus(pba + dtb_ref[...])
    row_in_chunk = lax.broadcasted_iota(jnp.int32, (n_rows, LANES), 0) % CHUNK
    shift = 1
    while shift < CHUNK:
        gam_c = gam_c + jnp.where(row_in_chunk >= shift, pltpu.roll(gam_c, shift, 0), 0.0)
        shift *= 2
    for h in range(DN_HEADS):
        gam_ref[:, h * hd:(h + 1) * hd] = jnp.broadcast_to(
            gam_c[:, DN_HEADS + h:DN_HEADS + h + 1], (n_rows, hd))
        betar_ref[:, h * hd:(h + 1) * hd] = jnp.broadcast_to(beta_c[:, h:h + 1], (n_rows, hd))

    ri = lax.broadcasted_iota(jnp.int32, (CHUNK, LANES), 0)
    lane = lax.broadcasted_iota(jnp.int32, (CHUNK, LANES), 1)
    ci = lane % CHUNK
    first = lane < CHUNK
    incl = ri >= ci
    strict = ri > ci
    same16 = (ri // 16) == (ci // 16)
    same32 = (ri // 32) == (ci // 32)
    m_bd16 = strict & same16
    m_e = strict & same32 & jnp.logical_not(same16)
    m_f = strict & jnp.logical_not(same32)
    zeros_h = jnp.zeros((CHUNK, hd), BF16)

    def pdot(x, y):
        ybd = _cat_rows([_bf(jnp.where(first, y, 0.0)), _bf(jnp.where(first, 0.0, y))])
        return jnp.dot(_bf(x), ybd, preferred_element_type=F32)

    chains = [(g, p) for g in range(nb * cpb) for p in range(DN_HEADS // 2)]
    lane2 = lax.broadcasted_iota(jnp.int32, (CHUNK, pw), 1)
    head0 = lane2 < hd

    def per_chain(fn, *cols):
        return [fn(*args) for args in zip(*cols)]

    def load(ref, col_off):
        return [ref[g * CHUNK:(g + 1) * CHUNK, col_off + p * pw:col_off + (p + 1) * pw] for g, p in chains]

    gr = load(gam_ref, 0)
    br = load(betar_ref, 0)
    q = load(qkv_ref, 0)
    k = load(qkv_ref, DN_WIDTH)

    def decay(gr_c):
        gcol = jnp.where(first, gr_c[:, 0:hd], gr_c[:, hd:pw])
        grow = _cat_rows([gr_c[:, 0:hd], gr_c[:, hd:pw]]).T[0:CHUNK, :]
        return jnp.where(incl, jnp.exp(jnp.minimum(gcol - grow, 0.0)), 0.0)

    dec = per_chain(decay, gr)
    qkkk = per_chain(
        lambda q_c, k_c: _dot_nt(
            _cat_rows([_bf(q_c), _bf(k_c)]),
            _cat_rows([_bf(jnp.where(head0, k_c, 0.0)), _bf(jnp.where(head0, 0.0, k_c))])),
        q, k)
    qkm = per_chain(lambda m, d: _bf(m[0:CHUNK, :] * d), qkkk, dec)
    m0 = per_chain(
        lambda m, d, br_c: -(jnp.where(first, br_c[:, 0:hd], br_c[:, hd:pw]) * m[CHUNK:, :] * d),
        qkkk, dec, br)

    a1 = per_chain(lambda m: jnp.where(m_bd16, m, 0.0), m0)
    a2 = per_chain(pdot, a1, a1)
    a4 = per_chain(pdot, a2, a2)
    a8 = per_chain(pdot, a4, a4)
    x = per_chain(lambda a, b_, c: a + b_ + c, a1, a2, per_chain(pdot, a1, a2))
    x = per_chain(lambda a, b_, c: a + b_ + c, x, a4, per_chain(pdot, x, a4))
    x = per_chain(lambda a, b_, c: a + b_ + c, x, a8, per_chain(pdot, x, a8))
    for mask in (m_e, m_f):
        e = per_chain(lambda m: jnp.where(mask, m, 0.0), m0)
        y = per_chain(lambda a, b_: a + b_, e, per_chain(pdot, x, e))
        x = per_chain(lambda a, b_, c: a + b_ + c, x, y, per_chain(pdot, y, x))

    v = load(qkv_ref, 2 * DN_WIDTH)
    egam = per_chain(jnp.exp, gr)
    bv = per_chain(lambda a, b_: a * b_, br, v)
    bk = per_chain(lambda a, e_, k_c: a * e_ * k_c, br, egam, k)

    def solve(x_c, bv_c, bk_c):
        rhs = _cat_lanes([bv_c[:, 0:hd], bk_c[:, 0:hd], bv_c[:, hd:pw], bk_c[:, hd:pw]])
        rhs_bd = _cat_rows([
            _cat_lanes([_bf(bv_c[:, 0:hd]), _bf(bk_c[:, 0:hd]), zeros_h, zeros_h]),
            _cat_lanes([zeros_h, zeros_h, _bf(bv_c[:, hd:pw]), _bf(bk_c[:, hd:pw])])])
        return rhs + jnp.dot(_bf(x_c), rhs_bd, preferred_element_type=F32)

    sol = per_chain(solve, x, bv, bk)
    u_part = per_chain(lambda s_: _cat_lanes([s_[:, 0:hd], s_[:, 2 * hd:3 * hd]]), sol)
    w_part = per_chain(lambda s_: _bf(_cat_lanes([s_[:, hd:2 * hd], s_[:, 3 * hd:4 * hd]])), sol)
    qd = per_chain(lambda q_c, e_: _bf(q_c * e_), q, egam)
    kd = per_chain(lambda k_c, g_c: _bf(k_c * jnp.exp(g_c[CHUNK - 1:CHUNK, :] - g_c)), k, gr)
    cd = per_chain(lambda g_c: jnp.exp(g_c[CHUNK - 1:CHUNK, :]), gr)

    dng = dng_ref[...]
    zeros_s = jnp.zeros((hd, hd), BF16)
    state = [[s_ref[b * DN_HEADS + h] for h in range(DN_HEADS)] for b in range(nb)]
    out_rows = [[None] * cpb for _ in range(nb)]
    pairs = [(b, p) for b in range(nb) for p in range(DN_HEADS // 2)]
    for j in range(cpb):
        idx = [chains.index((b * cpb + j, p)) for b, p in pairs]
        sbd = [_cat_rows([_cat_lanes([_bf(state[b][2 * p]), zeros_s]),
                          _cat_lanes([zeros_s, _bf(state[b][2 * p + 1])])]) for b, p in pairs]
        ws = [jnp.dot(_cat_rows([w_part[i], qd[i]]), s_, preferred_element_type=F32)
              for i, s_ in zip(idx, sbd)]
        ub = [_bf(u_part[i] - w_[0:CHUNK, :]) for i, w_ in zip(idx, ws)]
        ubd = [_cat_rows([_cat_lanes([u_[:, 0:hd], zeros_h]), _cat_lanes([zeros_h, u_[:, hd:pw]])]) for u_ in ub]
        o = [w_[CHUNK:, :] + jnp.dot(qkm[i], u_, preferred_element_type=F32)
             for i, w_, u_ in zip(idx, ws, ubd)]
        ktu = [[_dot_tn(kd[i][:, hh * hd:(hh + 1) * hd], u_[:, hh * hd:(hh + 1) * hd]) for hh in range(2)]
               for i, u_ in zip(idx, ub)]
        for n, (b, p) in enumerate(pairs):
            for hh in range(2):
                c0, c1 = hh * hd, (hh + 1) * hd
                state[b][2 * p + hh] = state[b][2 * p + hh] * cd[idx[n]][:, c0:c1] + ktu[n][hh]
        for b in range(nb):
            r0 = (b * cpb + j) * CHUNK
            gate = gate_ref[r0:r0 + CHUNK, :]
            o_b = _cat_lanes([o[pairs.index((b, p))] for p in range(DN_HEADS // 2)])
            normed = _cat_lanes([_rms_norm(o_b[:, h * hd:(h + 1) * hd], dng) for h in range(DN_HEADS)])
            out_rows[b][j] = _bf(normed * (gate * jax.nn.sigmoid(gate)))
    s_ref[...] = jnp.stack([state[b][h] for b in range(nb) for h in range(DN_HEADS)], axis=0)
    w_out_c = wout_ref[GM_WIDTH + CV_WIDTH:, :]
    for b in range(nb):
        out_ref[b] += jnp.dot(_cat_rows(out_rows[b]), w_out_c, preferred_element_type=F32)


def _mix(x3d, prm):
    b, t, d = x3d.shape
    nb, tb = NB_MIX, TB_MIX
    n_rows = nb * tb

    def full(a):
        nd = a.ndim
        return pl.BlockSpec(a.shape, lambda i, j, _nd=nd: (0,) * _nd)

    params = [prm["g_in"], prm["w_in"], prm["w_out"],
              prm["gm_ln_g"], prm["gm_ln_b"], prm["gm_wcat"], prm["gm_bias"],
              prm["cv_w"], prm["cv_b"], prm["cv_ln_g"], prm["cv_ln_b"],
              prm["dn_w"], prm["dn_alog"], prm["dn_dtb"], prm["dn_norm_g"]]
    return pl.pallas_call(
        _mix_kernel,
        grid=(b // nb, t // tb),
        in_specs=[pl.BlockSpec((nb, tb, d), lambda i, j: (i, j, 0))] + [full(a) for a in params],
        out_specs=pl.BlockSpec((nb, tb, d), lambda i, j: (i, j, 0)),
        out_shape=jax.ShapeDtypeStruct((b, t, d), F32),
        scratch_shapes=[
            pltpu.VMEM((nb, tb, d), BF16),
            pltpu.VMEM((nb, tb, 2 * GM_WIDTH), F32),
            pltpu.VMEM((nb, tb, 2 * CV_WIDTH), F32),
            pltpu.VMEM((CV_HALO + tb + CV_TAIL, CV_WIDTH), F32),
            pltpu.VMEM((nb, CV_HALO, CV_WIDTH), F32),
            pltpu.VMEM((nb, DN_HALO + tb, 3 * DN_WIDTH), F32),
            pltpu.VMEM((nb, DN_HALO, 3 * DN_WIDTH), F32),
            pltpu.VMEM((n_rows, 3 * DN_WIDTH), F32),
            pltpu.VMEM((n_rows, DN_WIDTH), F32),
            pltpu.VMEM((n_rows, LANES), F32),
            pltpu.VMEM((n_rows, DN_WIDTH), F32),
            pltpu.VMEM((n_rows, DN_WIDTH), F32),
            pltpu.VMEM((nb * DN_HEADS, DN_HEAD_DIM, DN_HEAD_DIM), F32),
        ],
        compiler_params=pltpu.CompilerParams(
            dimension_semantics=("arbitrary", "arbitrary"),
            vmem_limit_bytes=56 * 1024 * 1024),
    )(x3d, *params)


def _pad_rows(a, rows):
    return jnp.pad(a, ((0, rows - a.shape[0]), (0, 0)))


def kernel(x, mix_norm_g, w_in, gm_ln_g, gm_ln_b, gm_ws, gm_bs, cv_dw_w, cv_dw_b, cv_ln_g, cv_ln_b, dn_conv_w, dn_a_log, dn_dt_bias, dn_norm_g, w_out, ffn_norm_g, w_gate, w_up, w_down, final_norm_g):
    b, t, d = x.shape
    depth = w_in.shape[0]
    d_proj = w_in.shape[2]
    d_ff = w_gate.shape[2]
    assert t % TB_MIX == 0 and b % NB_MIX == 0 and TB_MIX == GM_BLOCK
    assert (b * t) % TM_FFN == 0
    assert d_ff % FF_CHUNK == 0 and d_proj == OFF_DN_BA + 2 * DN_HEADS

    for l in range(depth):
        lane_pad = (0, LANES - 2 * DN_HEADS)
        prm = {
            "g_in": mix_norm_g[l][None, :],
            "w_in": jnp.pad(w_in[l], ((0, 0), (0, D_PROJ_PAD - d_proj))).astype(BF16),
            "w_out": w_out[l].astype(BF16),
            "gm_ln_g": gm_ln_g[l][None, :], "gm_ln_b": gm_ln_b[l][None, :],
            "gm_wcat": gm_ws[l].transpose(1, 0, 2).reshape(GM_BLOCK, GM_HEADS * GM_BLOCK),
            "gm_bias": jnp.repeat(gm_bs[l].T, GM_HEAD_DIM, axis=1),
            "cv_w": _pad_rows(cv_dw_w[l], 32), "cv_b": cv_dw_b[l][None, :],
            "cv_ln_g": cv_ln_g[l][None, :], "cv_ln_b": cv_ln_b[l][None, :],
            "dn_w": _pad_rows(dn_conv_w[l], 8),
            "dn_alog": jnp.pad(jnp.concatenate([jnp.zeros((DN_HEADS,), F32), dn_a_log[l]]), lane_pad)[None, :],
            "dn_dtb": jnp.pad(jnp.concatenate([jnp.zeros((DN_HEADS,), F32), dn_dt_bias[l]]), lane_pad)[None, :],
            "dn_norm_g": dn_norm_g[l][None, :],
        }
        x = _mix(x, prm)
        x2d = _ffn(x.reshape(b * t, d), ffn_norm_g[l][None, :], w_gate[l].astype(BF16),
                   w_up[l].astype(BF16), w_down[l].astype(BF16), final_norm_g[None, :],
                   final_norm=(l == depth - 1))
        x = x2d.reshape(b, t, d)
    return x
```

```python
import functools

import jax
import jax.numpy as jnp
from jax import lax
from jax.experimental import pallas as pl
from jax.experimental.pallas import tpu as pltpu

F32 = jnp.float32
BF16 = jnp.bfloat16
EPS = 1e-6

LANES = 128
V7X_VMEM_BYTES = 64 * 1024 * 1024

GM_WIDTH = 256
GM_HEADS = 4
GM_HEAD_DIM = GM_WIDTH // GM_HEADS
GM_BLOCK = 128
CV_WIDTH = 256
CV_KERNEL = 31
DN_WIDTH = 512
DN_HEAD_DIM = 128
DN_HEADS = 4
DN_CONV = 4
CHUNK = 64

OFF_GM_U = 0
OFF_GM_V = OFF_GM_U + GM_WIDTH
OFF_CV_A = OFF_GM_V + GM_WIDTH
OFF_CV_G = OFF_CV_A + CV_WIDTH
OFF_DN_QKV = OFF_CV_G + CV_WIDTH
OFF_DN_GATE = OFF_DN_QKV + 3 * DN_WIDTH
OFF_DN_BA = OFF_DN_GATE + DN_WIDTH
D_PROJ_PAD = OFF_DN_BA + LANES

TM_FFN = 512
TB_MIX = 128
NB_MIX = 8
FF_CHUNK = 256
CV_HALO = 32
DN_HALO = 8
CV_TAIL = 8


def _rms_norm(x, g):
    ms = jnp.mean(x * x, axis=-1, keepdims=True)
    return x * lax.rsqrt(ms + EPS) * g


def _layer_norm(x, g, b):
    mu = jnp.mean(x, axis=-1, keepdims=True)
    xc = x - mu
    var = jnp.mean(xc * xc, axis=-1, keepdims=True)
    return xc * lax.rsqrt(var + EPS) * g + b


def _gelu(x):
    return 0.5 * x * (1.0 + lax.erf(x * (2.0 ** -0.5)))


def _dot_nt(a, b):
    return lax.dot_general(a.astype(BF16), b.astype(BF16), (((1,), (1,)), ((), ())),
                           preferred_element_type=F32)


def _dot_tn(a, b):
    return lax.dot_general(a.astype(BF16), b.astype(BF16), (((0,), (0,)), ((), ())),
                           preferred_element_type=F32)


def _ffn_kernel(x_ref, g_ref, wg_ref, wu_ref, wd_ref, gf_ref, o_ref, *, final_norm):
    x = x_ref[...]
    h = _rms_norm(x, g_ref[...]).astype(BF16)
    acc = x
    d_ff = wg_ref.shape[1]
    for c0 in range(0, d_ff, FF_CHUNK):
        c1 = c0 + FF_CHUNK
        gate = jnp.dot(h, wg_ref[:, c0:c1], preferred_element_type=F32)
        up = jnp.dot(h, wu_ref[:, c0:c1], preferred_element_type=F32)
        act = (gate * jax.nn.sigmoid(gate) * up).astype(BF16)
        acc = acc + jnp.dot(act, wd_ref[c0:c1, :], preferred_element_type=F32)
    if final_norm:
        acc = _rms_norm(acc, gf_ref[...])
    o_ref[...] = acc


def _ffn(x2d, g, wg, wu, wd, gf, final_norm):
    n_tok, d = x2d.shape
    const2 = lambda i: (0, 0)
    return pl.pallas_call(
        functools.partial(_ffn_kernel, final_norm=final_norm),
        grid=(n_tok // TM_FFN,),
        in_specs=[
            pl.BlockSpec((TM_FFN, d), lambda i: (i, 0)),
            pl.BlockSpec((1, d), const2),
            pl.BlockSpec(wg.shape, const2),
            pl.BlockSpec(wu.shape, const2),
            pl.BlockSpec(wd.shape, const2),
            pl.BlockSpec((1, d), const2),
        ],
        out_specs=pl.BlockSpec((TM_FFN, d), lambda i: (i, 0)),
        out_shape=jax.ShapeDtypeStruct((n_tok, d), F32),
        compiler_params=pltpu.CompilerParams(
            dimension_semantics=("arbitrary",),
            vmem_limit_bytes=56 * 1024 * 1024),
    )(x2d, g, wg, wu, wd, gf)


def _bf(x):
    return x.astype(BF16)


def _cat_rows(xs):
    return jnp.concatenate(xs, axis=0)


def _cat_lanes(xs):
    return jnp.concatenate(xs, axis=1)


def _mix_kernel(x_ref, g_in_ref, win_ref, wout_ref,
                gm_g_ref, gm_b_ref, wcat_ref, gm_bias_ref,
                cvw_ref, cvb_ref, cvg_ref, cvbeta_ref,
                dnw_ref, alog_ref, dtb_ref, dng_ref,
                out_ref,
                h_ref, pg_ref, pc_ref, ypad_ref, yhalo_ref, dnpad_ref, dnhalo_ref, qkv_ref, gate_ref, pba_ref,
                gam_ref, betar_ref, s_ref):
    t = pl.program_id(1)
    nb, tb = out_ref.shape[0], out_ref.shape[1]
    n_rows = nb * tb
    cpb = tb // CHUNK
    hd = DN_HEAD_DIM
    pw = 2 * hd

    @pl.when(t == 0)
    def _():
        yhalo_ref[...] = jnp.zeros(yhalo_ref.shape, F32)
        dnhalo_ref[...] = jnp.zeros(dnhalo_ref.shape, F32)
        s_ref[...] = jnp.zeros(s_ref.shape, F32)
        ypad_ref[CV_HALO + tb:, :] = jnp.zeros((CV_TAIL, CV_WIDTH), F32)

    row_i = lax.broadcasted_iota(jnp.int32, (GM_BLOCK, GM_HEADS * GM_BLOCK), 0)
    col_i = lax.broadcasted_iota(jnp.int32, (GM_BLOCK, GM_HEADS * GM_BLOCK), 1) % GM_BLOCK
    w_cat = _bf(jnp.where(row_i >= col_i, wcat_ref[...], 0.0))
    head_of_lane = lax.broadcasted_iota(jnp.int32, (GM_BLOCK, GM_WIDTH), 1) // GM_HEAD_DIM
    gm_g = gm_g_ref[...]
    gm_b = gm_b_ref[...]
    cvb = cvb_ref[...]
    cvg = cvg_ref[...]
    cvbeta = cvbeta_ref[...]
    cv_lead = CV_HALO - (CV_KERNEL - 1)
    dn_lead = DN_HALO - (DN_CONV - 1)
    slab = tb + 8
    g_in = g_in_ref[...]

    def proj(hb, c0, c1):
        return jnp.dot(hb, win_ref[:, c0:c1], preferred_element_type=F32)

    for b in range(nb):
        h_ref[b] = _bf(_rms_norm(x_ref[b], g_in))
        pg_ref[b] = proj(h_ref[b], OFF_GM_U, OFF_CV_A)

    ya = []
    for b in range(nb):
        pc_ref[b] = proj(h_ref[b], OFF_CV_A, OFF_DN_QKV)
        u = _gelu(pg_ref[b, :, 0:GM_WIDTH])
        vn = _layer_norm(_gelu(pg_ref[b, :, GM_WIDTH:]), gm_g, gm_b)
        vbd = _cat_rows([_bf(jnp.where(head_of_lane == h, vn, 0.0)) for h in range(GM_HEADS)])
        s = gm_bias_ref[...] + jnp.dot(w_cat, vbd, preferred_element_type=F32)
        ya.append(_bf(u * s))

    for b in range(nb):
        dnpad_ref[b, DN_HALO:DN_HALO + tb, :] = proj(h_ref[b], OFF_DN_QKV, OFF_DN_GATE)
        acc_x = x_ref[b] + jnp.dot(ya[b], wout_ref[0:GM_WIDTH, :], preferred_element_type=F32)
        ypad_ref[0:CV_HALO, :] = yhalo_ref[b]
        ypad_ref[CV_HALO:CV_HALO + tb, :] = (
            pc_ref[b, :, 0:CV_WIDTH] * jax.nn.sigmoid(pc_ref[b, :, CV_WIDTH:]))
        acc = jnp.broadcast_to(cvb, (tb, CV_WIDTH))
        for r in range(8):
            z = None
            for s_off in range(cv_lead, cv_lead + CV_KERNEL):
                if s_off % 8 != r:
                    continue
                k = s_off - cv_lead
                a8 = (s_off // 8) * 8
                term = cvw_ref[k:k + 1, :] * ypad_ref[a8:a8 + slab, :]
                z = term if z is None else z + term
            if z is not None:
                acc = acc + z[r:r + tb, :]
        yhalo_ref[b] = ypad_ref[tb:tb + CV_HALO, :]
        zn = _layer_norm(acc, cvg, cvbeta)
        out_ref[b] = acc_x + jnp.dot(_bf(zn * jax.nn.sigmoid(zn)),
                                     wout_ref[GM_WIDTH:GM_WIDTH + CV_WIDTH, :], preferred_element_type=F32)

    for b in range(nb):
        pgb = proj(h_ref[b], OFF_DN_GATE, D_PROJ_PAD)
        gate_ref[b * tb:(b + 1) * tb, :] = pgb[:, 0:DN_WIDTH]
        pba_ref[b * tb:(b + 1) * tb, :] = pgb[:, DN_WIDTH:]
        dnpad_ref[b, 0:DN_HALO, :] = dnhalo_ref[b]
        for c in range(3 * DN_HEADS):
            c0, c1 = c * hd, (c + 1) * hd
            acc = dnw_ref[0:1, c0:c1] * dnpad_ref[b, dn_lead:dn_lead + tb, c0:c1]
            for k in range(1, DN_CONV):
                acc = acc + dnw_ref[k:k + 1, c0:c1] * dnpad_ref[b, dn_lead + k:dn_lead + k + tb, c0:c1]
            z = acc * jax.nn.sigmoid(acc)
            if c < 2 * DN_HEADS:
                z = z * lax.rsqrt(jnp.sum(z * z, axis=-1, keepdims=True) + EPS)
                if c < DN_HEADS:
                    z = z * (hd ** -0.5)
            qkv_ref[b * tb:(b + 1) * tb, c0:c1] = z
        dnhalo_ref[b] = dnpad_ref[b, tb:tb + DN_HALO, :]

    pba = pba_ref[...]
    beta_c = jax.nn.sigmoid(pba)
    gam_c = -jnp.exp(alog_ref[...]) * jax.nn.softplus(pba + dtb_ref[...])
    row_in_chunk = lax.broadcasted_iota(jnp.int32, (n_rows, LANES), 0) % CHUNK
    shift = 1
    while shift < CHUNK:
        gam_c = gam_c + jnp.where(row_in_chunk >= shift, pltpu.roll(gam_c, shift, 0), 0.0)
        shift *= 2
    for h in range(DN_HEADS):
        gam_ref[:, h * hd:(h + 1) * hd] = jnp.broadcast_to(
            gam_c[:, DN_HEADS + h:DN_HEADS + h + 1], (n_rows, hd))
        betar_ref[:, h * hd:(h + 1) * hd] = jnp.broadcast_to(beta_c[:, h:h + 1], (n_rows, hd))

    ri = lax.broadcasted_iota(jnp.int32, (CHUNK, LANES), 0)
    lane = lax.broadcasted_iota(jnp.int32, (CHUNK, LANES), 1)
    ci = lane % CHUNK
    first = lane < CHUNK
    incl = ri >= ci
    strict = ri > ci
    same16 = (ri // 16) == (ci // 16)
    same32 = (ri // 32) == (ci // 32)
    m_bd16 = strict & same16
    m_e = strict & same32 & jnp.logical_not(same16)
    m_f = strict & jnp.logical_not(same32)
    zeros_h = jnp.zeros((CHUNK, hd), BF16)

    def pdot(x, y):
        ybd = _cat_rows([_bf(jnp.where(first, y, 0.0)), _bf(jnp.where(first, 0.0, y))])
        return jnp.dot(_bf(x), ybd, preferred_element_type=F32)

    chains = [(g, p) for g in range(nb * cpb) for p in range(DN_HEADS // 2)]
    lane2 = lax.broadcasted_iota(jnp.int32, (CHUNK, pw), 1)
    head0 = lane2 < hd

    def per_chain(fn, *cols):
        return [fn(*args) for args in zip(*cols)]

    def load(ref, col_off):
        return [ref[g * CHUNK:(g + 1) * CHUNK, col_off + p * pw:col_off + (p + 1) * pw] for g, p in chains]

    gr = load(gam_ref, 0)
    br = load(betar_ref, 0)
    q = load(qkv_ref, 0)
    k = load(qkv_ref, DN_WIDTH)

    def decay(gr_c):
        gcol = jnp.where(first, gr_c[:, 0:hd], gr_c[:, hd:pw])
        grow = _cat_rows([gr_c[:, 0:hd], gr_c[:, hd:pw]]).T[0:CHUNK, :]
        return jnp.where(incl, jnp.exp(jnp.minimum(gcol - grow, 0.0)), 0.0)

    dec = per_chain(decay, gr)
    qkkk = per_chain(
        lambda q_c, k_c: _dot_nt(
            _cat_rows([_bf(q_c), _bf(k_c)]),
            _cat_rows([_bf(jnp.where(head0, k_c, 0.0)), _bf(jnp.where(head0, 0.0, k_c))])),
        q, k)
    qkm = per_chain(lambda m, d: _bf(m[0:CHUNK, :] * d), qkkk, dec)
    m0 = per_chain(
        lambda m, d, br_c: -(jnp.where(first, br_c[:, 0:hd], br_c[:, hd:pw]) * m[CHUNK:, :] * d),
        qkkk, dec, br)

    a1 = per_chain(lambda m: jnp.where(m_bd16, m, 0.0), m0)
    a2 = per_chain(pdot, a1, a1)
    a4 = per_chain(pdot, a2, a2)
    a8 = per_chain(pdot, a4, a4)
    x = per_chain(lambda a, b_, c: a + b_ + c, a1, a2, per_chain(pdot, a1, a2))
    x = per_chain(lambda a, b_, c: a + b_ + c, x, a4, per_chain(pdot, x, a4))
    x = per_chain(lambda a, b_, c: a + b_ + c, x, a8, per_chain(pdot, x, a8))
    for mask in (m_e, m_f):
        e = per_chain(lambda m: jnp.where(mask, m, 0.0), m0)
        y = per_chain(lambda a, b_: a + b_, e, per_chain(pdot, x, e))
        x = per_chain(lambda a, b_, c: a + b_ + c, x, y, per_chain(pdot, y, x))

    v = load(qkv_ref, 2 * DN_WIDTH)
    egam = per_chain(jnp.exp, gr)
    bv = per_chain(lambda a, b_: a * b_, br, v)
    bk = per_chain(lambda a, e_, k_c: a * e_ * k_c, br, egam, k)

    def solve(x_c, bv_c, bk_c):
        rhs = _cat_lanes([bv_c[:, 0:hd], bk_c[:, 0:hd], bv_c[:, hd:pw], bk_c[:, hd:pw]])
        rhs_bd = _cat_rows([
            _cat_lanes([_bf(bv_c[:, 0:hd]), _bf(bk_c[:, 0:hd]), zeros_h, zeros_h]),
            _cat_lanes([zeros_h, zeros_h, _bf(bv_c[:, hd:pw]), _bf(bk_c[:, hd:pw])])])
        return rhs + jnp.dot(_bf(x_c), rhs_bd, preferred_element_type=F32)

    sol = per_chain(solve, x, bv, bk)
    u_part = per_chain(lambda s_: _cat_lanes([s_[:, 0:hd], s_[:, 2 * hd:3 * hd]]), sol)
    w_part = per_chain(lambda s_: _bf(_cat_lanes([s_[:, hd:2 * hd], s_[:, 3 * hd:4 * hd]])), sol)
    qd = per_chain(lambda q_c, e_: _bf(q_c * e_), q, egam)
    kd = per_chain(lambda k_c, g_c: _bf(k_c * jnp.exp(g_c[CHUNK - 1:CHUNK, :] - g_c)), k, gr)
    cd = per_chain(lambda g_c: jnp.exp(g_c[CHUNK - 1:CHUNK, :]), gr)

    dng = dng_ref[...]
    zeros_s = jnp.zeros((hd, hd), BF16)
    state = [[s_ref[b * DN_HEADS + h] for h in range(DN_HEADS)] for b in range(nb)]
    out_rows = [[None] * cpb for _ in range(nb)]
    pairs = [(b, p) for b in range(nb) for p in range(DN_HEADS // 2)]
    for j in range(cpb):
        idx = [chains.index((b * cpb + j, p)) for b, p in pairs]
        sbd = [_cat_rows([_cat_lanes([_bf(state[b][2 * p]), zeros_s]),
                          _cat_lanes([zeros_s, _bf(state[b][2 * p + 1])])]) for b, p in pairs]
        ws = [jnp.dot(_cat_rows([w_part[i], qd[i]]), s_, preferred_element_type=F32)
              for i, s_ in zip(idx, sbd)]
        ub = [_bf(u_part[i] - w_[0:CHUNK, :]) for i, w_ in zip(idx, ws)]
        ubd = [_cat_rows([_cat_lanes([u_[:, 0:hd], zeros_h]), _cat_lanes([zeros_h, u_[:, hd:pw]])]) for u_ in ub]
        o = [w_[CHUNK:, :] + jnp.dot(qkm[i], u_, preferred_element_type=F32)
             for i, w_, u_ in zip(idx, ws, ubd)]
        ktu = [[_dot_tn(kd[i][:, hh * hd:(hh + 1) * hd], u_[:, hh * hd:(hh + 1) * hd]) for hh in range(2)]
               for i, u_ in zip(idx, ub)]
        for n, (b, p) in enumerate(pairs):
            for hh in range(2):
                c0, c1 = hh * hd, (hh + 1) * hd
                state[b][2 * p + hh] = state[b][2 * p + hh] * cd[idx[n]][:, c0:c1] + ktu[n][hh]
        for b in range(nb):
            r0 = (b * cpb + j) * CHUNK
            gate = gate_ref[r0:r0 + CHUNK, :]
            o_b = _cat_lanes([o[pairs.index((b, p))] for p in range(DN_HEADS // 2)])
            normed = _cat_lanes([_rms_norm(o_b[:, h * hd:(h + 1) * hd], dng) for h in range(DN_HEADS)])
            out_rows[b][j] = _bf(normed * (gate * jax.nn.sigmoid(gate)))
    s_ref[...] = jnp.stack([state[b][h] for b in range(nb) for h in range(DN_HEADS)], axis=0)
    w_out_c = wout_ref[GM_WIDTH + CV_WIDTH:, :]
    for b in range(nb):
        out_ref[b] += jnp.dot(_cat_rows(out_rows[b]), w_out_c, preferred_element_type=F32)


def _mix(x3d, prm):
    b, t, d = x3d.shape
    nb, tb = NB_MIX, TB_MIX
    n_rows = nb * tb

    def full(a):
        nd = a.ndim
        return pl.BlockSpec(a.shape, lambda i, j, _nd=nd: (0,) * _nd, pipeline_mode=pl.Buffered(1))

    params = [prm["g_in"], prm["w_in"], prm["w_out"],
              prm["gm_ln_g"], prm["gm_ln_b"], prm["gm_wcat"], prm["gm_bias"],
              prm["cv_w"], prm["cv_b"], prm["cv_ln_g"], prm["cv_ln_b"],
              prm["dn_w"], prm["dn_alog"], prm["dn_dtb"], prm["dn_norm_g"]]
    return pl.pallas_call(
        _mix_kernel,
        grid=(b // nb, t // tb),
        in_specs=[pl.BlockSpec((nb, tb, d), lambda i, j: (i, j, 0))] + [full(a) for a in params],
        out_specs=pl.BlockSpec((nb, tb, d), lambda i, j: (i, j, 0)),
        out_shape=jax.ShapeDtypeStruct((b, t, d), F32),
        scratch_shapes=[
            pltpu.VMEM((nb, tb, d), BF16),
            pltpu.VMEM((nb, tb, 2 * GM_WIDTH), F32),
            pltpu.VMEM((nb, tb, 2 * CV_WIDTH), F32),
            pltpu.VMEM((CV_HALO + tb + CV_TAIL, CV_WIDTH), F32),
            pltpu.VMEM((nb, CV_HALO, CV_WIDTH), F32),
            pltpu.VMEM((nb, DN_HALO + tb, 3 * DN_WIDTH), F32),
            pltpu.VMEM((nb, DN_HALO, 3 * DN_WIDTH), F32),
            pltpu.VMEM((n_rows, 3 * DN_WIDTH), F32),
            pltpu.VMEM((n_rows, DN_WIDTH), F32),
            pltpu.VMEM((n_rows, LANES), F32),
            pltpu.VMEM((n_rows, DN_WIDTH), F32),
            pltpu.VMEM((n_rows, DN_WIDTH), F32),
            pltpu.VMEM((nb * DN_HEADS, DN_HEAD_DIM, DN_HEAD_DIM), F32),
        ],
        compiler_params=pltpu.CompilerParams(
            dimension_semantics=("arbitrary", "arbitrary"),
            vmem_limit_bytes=56 * 1024 * 1024),
    )(x3d, *params)


def _pad_rows(a, rows):
    return jnp.pad(a, ((0, rows - a.shape[0]), (0, 0)))


def kernel(x, mix_norm_g, w_in, gm_ln_g, gm_ln_b, gm_ws, gm_bs, cv_dw_w, cv_dw_b, cv_ln_g, cv_ln_b, dn_conv_w, dn_a_log, dn_dt_bias, dn_norm_g, w_out, ffn_norm_g, w_gate, w_up, w_down, final_norm_g):
    b, t, d = x.shape
    depth = w_in.shape[0]
    d_proj = w_in.shape[2]
    d_ff = w_gate.shape[2]
    assert t % TB_MIX == 0 and b % NB_MIX == 0 and TB_MIX == GM_BLOCK
    assert (b * t) % TM_FFN == 0
    assert d_ff % FF_CHUNK == 0 and d_proj == OFF_DN_BA + 2 * DN_HEADS

    for l in range(depth):
        lane_pad = (0, LANES - 2 * DN_HEADS)
        prm = {
            "g_in": mix_norm_g[l][None, :],
            "w_in": jnp.pad(w_in[l], ((0, 0), (0, D_PROJ_PAD - d_proj))).astype(BF16),
            "w_out": w_out[l].astype(BF16),
            "gm_ln_g": gm_ln_g[l][None, :], "gm_ln_b": gm_ln_b[l][None, :],
            "gm_wcat": gm_ws[l].transpose(1, 0, 2).reshape(GM_BLOCK, GM_HEADS * GM_BLOCK),
            "gm_bias": jnp.repeat(gm_bs[l].T, GM_HEAD_DIM, axis=1),
            "cv_w": _pad_rows(cv_dw_w[l], 32), "cv_b": cv_dw_b[l][None, :],
            "cv_ln_g": cv_ln_g[l][None, :], "cv_ln_b": cv_ln_b[l][None, :],
            "dn_w": _pad_rows(dn_conv_w[l], 8),
            "dn_alog": jnp.pad(jnp.concatenate([jnp.zeros((DN_HEADS,), F32), dn_a_log[l]]), lane_pad)[None, :],
            "dn_dtb": jnp.pad(jnp.concatenate([jnp.zeros((DN_HEADS,), F32), dn_dt_bias[l]]), lane_pad)[None, :],
            "dn_norm_g": dn_norm_g[l][None, :],
        }
        x = _mix(x, prm)
        x2d = _ffn(x.reshape(b * t, d), ffn_norm_g[l][None, :], w_gate[l].astype(BF16),
                   w_up[l].astype(BF16), w_down[l].astype(BF16), final_norm_g[None, :],
                   final_norm=(l == depth - 1))
        x = x2d.reshape(b, t, d)
    return x
```

```python
import functools

import jax
import jax.numpy as jnp
from jax import lax
from jax.experimental import pallas as pl
from jax.experimental.pallas import tpu as pltpu

F32 = jnp.float32
BF16 = jnp.bfloat16
EPS = 1e-6

LANES = 128
V7X_VMEM_BYTES = 64 * 1024 * 1024

GM_WIDTH = 256
GM_HEADS = 4
GM_HEAD_DIM = GM_WIDTH // GM_HEADS
GM_BLOCK = 128
CV_WIDTH = 256
CV_KERNEL = 31
DN_WIDTH = 512
DN_HEAD_DIM = 128
DN_HEADS = 4
DN_CONV = 4
CHUNK = 64

OFF_GM_U = 0
OFF_GM_V = OFF_GM_U + GM_WIDTH
OFF_CV_A = OFF_GM_V + GM_WIDTH
OFF_CV_G = OFF_CV_A + CV_WIDTH
OFF_DN_QKV = OFF_CV_G + CV_WIDTH
OFF_DN_GATE = OFF_DN_QKV + 3 * DN_WIDTH
OFF_DN_BA = OFF_DN_GATE + DN_WIDTH
D_PROJ_PAD = OFF_DN_BA + LANES

TM_FFN = 512
TB_MIX = 128
NB_MIX = 8
FF_CHUNK = 256
CV_HALO = 32
DN_HALO = 16
CV_TAIL = 8


def _rms_norm(x, g):
    ms = jnp.mean(x * x, axis=-1, keepdims=True)
    return x * lax.rsqrt(ms + EPS) * g


def _layer_norm(x, g, b):
    mu = jnp.mean(x, axis=-1, keepdims=True)
    xc = x - mu
    var = jnp.mean(xc * xc, axis=-1, keepdims=True)
    return xc * lax.rsqrt(var + EPS) * g + b


def _gelu(x):
    return 0.5 * x * (1.0 + lax.erf(x * (2.0 ** -0.5)))


def _dot_nt(a, b):
    return lax.dot_general(a.astype(BF16), b.astype(BF16), (((1,), (1,)), ((), ())),
                           preferred_element_type=F32)


def _dot_tn(a, b):
    return lax.dot_general(a.astype(BF16), b.astype(BF16), (((0,), (0,)), ((), ())),
                           preferred_element_type=F32)


def _ffn_kernel(x_ref, g_ref, wg_ref, wu_ref, wd_ref, gf_ref, o_ref, *, final_norm):
    x = x_ref[...]
    h = _rms_norm(x, g_ref[...]).astype(BF16)
    acc = x
    d_ff = wg_ref.shape[1]
    for c0 in range(0, d_ff, FF_CHUNK):
        c1 = c0 + FF_CHUNK
        gate = jnp.dot(h, wg_ref[:, c0:c1], preferred_element_type=F32)
        up = jnp.dot(h, wu_ref[:, c0:c1], preferred_element_type=F32)
        act = (gate * jax.nn.sigmoid(gate) * up).astype(BF16)
        acc = acc + jnp.dot(act, wd_ref[c0:c1, :], preferred_element_type=F32)
    if final_norm:
        acc = _rms_norm(acc, gf_ref[...])
    o_ref[...] = acc


def _ffn(x2d, g, wg, wu, wd, gf, final_norm):
    n_tok, d = x2d.shape
    const2 = lambda i: (0, 0)
    return pl.pallas_call(
        functools.partial(_ffn_kernel, final_norm=final_norm),
        grid=(n_tok // TM_FFN,),
        in_specs=[
            pl.BlockSpec((TM_FFN, d), lambda i: (i, 0)),
            pl.BlockSpec((1, d), const2),
            pl.BlockSpec(wg.shape, const2),
            pl.BlockSpec(wu.shape, const2),
            pl.BlockSpec(wd.shape, const2),
            pl.BlockSpec((1, d), const2),
        ],
        out_specs=pl.BlockSpec((TM_FFN, d), lambda i: (i, 0)),
        out_shape=jax.ShapeDtypeStruct((n_tok, d), F32),
        compiler_params=pltpu.CompilerParams(
            dimension_semantics=("arbitrary",),
            vmem_limit_bytes=56 * 1024 * 1024),
    )(x2d, g, wg, wu, wd, gf)


def _bf(x):
    return x.astype(BF16)


def _cat_rows(xs):
    return jnp.concatenate(xs, axis=0)


def _cat_lanes(xs):
    return jnp.concatenate(xs, axis=1)


def _mix_kernel(x_ref, g_in_ref, win_ref, wout_ref,
                gm_g_ref, gm_b_ref, wcat_ref, gm_bias_ref,
                cvw_ref, cvb_ref, cvg_ref, cvbeta_ref,
                dnw_ref, alog_ref, dtb_ref, dng_ref,
                out_ref,
                h_ref, pg_ref, pc_ref, ypad_ref, yhalo_ref, dnpad_ref, dnhalo_ref, qkv_ref, gate_ref, pba_ref,
                gam_ref, betar_ref, s_ref):
    t = pl.program_id(1)
    nb, tb = out_ref.shape[0], out_ref.shape[1]
    n_rows = nb * tb
    cpb = tb // CHUNK
    hd = DN_HEAD_DIM
    pw = 2 * hd

    @pl.when(t == 0)
    def _():
        yhalo_ref[...] = jnp.zeros(yhalo_ref.shape, F32)
        dnhalo_ref[...] = jnp.zeros(dnhalo_ref.shape, F32)
        s_ref[...] = jnp.zeros(s_ref.shape, F32)
        ypad_ref[CV_HALO + tb:, :] = jnp.zeros((CV_TAIL, CV_WIDTH), F32)

    row_i = lax.broadcasted_iota(jnp.int32, (GM_BLOCK, GM_HEADS * GM_BLOCK), 0)
    col_i = lax.broadcasted_iota(jnp.int32, (GM_BLOCK, GM_HEADS * GM_BLOCK), 1) % GM_BLOCK
    w_cat = _bf(jnp.where(row_i >= col_i, wcat_ref[...], 0.0))
    head_of_lane = lax.broadcasted_iota(jnp.int32, (GM_BLOCK, GM_WIDTH), 1) // GM_HEAD_DIM
    gm_g = gm_g_ref[...]
    gm_b = gm_b_ref[...]
    cvb = cvb_ref[...]
    cvg = cvg_ref[...]
    cvbeta = cvbeta_ref[...]
    cv_lead = CV_HALO - (CV_KERNEL - 1)
    slab = tb + 8
    g_in = g_in_ref[...]

    def proj(hb, c0, c1):
        return jnp.dot(hb, win_ref[:, c0:c1], preferred_element_type=F32)

    for b in range(nb):
        h_ref[b] = _bf(_rms_norm(x_ref[b], g_in))
        pg_ref[b] = proj(h_ref[b], OFF_GM_U, OFF_CV_A)

    ya = []
    for b in range(nb):
        pc_ref[b] = proj(h_ref[b], OFF_CV_A, OFF_DN_QKV)
        u = _gelu(pg_ref[b, :, 0:GM_WIDTH])
        vn = _layer_norm(_gelu(pg_ref[b, :, GM_WIDTH:]), gm_g, gm_b)
        vbd = _cat_rows([_bf(jnp.where(head_of_lane == h, vn, 0.0)) for h in range(GM_HEADS)])
        s = gm_bias_ref[...] + jnp.dot(w_cat, vbd, preferred_element_type=F32)
        ya.append(_bf(u * s))

    for b in range(nb):
        dnpad_ref[b, DN_HALO:DN_HALO + tb, :] = proj(h_ref[b], OFF_DN_QKV, OFF_DN_GATE)
        acc_x = x_ref[b] + jnp.dot(ya[b], wout_ref[0:GM_WIDTH, :], preferred_element_type=F32)
        ypad_ref[0:CV_HALO, :] = yhalo_ref[b]
        ypad_ref[CV_HALO:CV_HALO + tb, :] = (
            pc_ref[b, :, 0:CV_WIDTH] * jax.nn.sigmoid(pc_ref[b, :, CV_WIDTH:]))
        acc = jnp.broadcast_to(cvb, (tb, CV_WIDTH))
        for r in range(8):
            z = None
            for s_off in range(cv_lead, cv_lead + CV_KERNEL):
                if s_off % 8 != r:
                    continue
                k = s_off - cv_lead
                a8 = (s_off // 8) * 8
                term = cvw_ref[k:k + 1, :] * ypad_ref[a8:a8 + slab, :]
                z = term if z is None else z + term
            if z is not None:
                acc = acc + z[r:r + tb, :]
        yhalo_ref[b] = ypad_ref[tb:tb + CV_HALO, :]
        zn = _layer_norm(acc, cvg, cvbeta)
        out_ref[b] = acc_x + jnp.dot(_bf(zn * jax.nn.sigmoid(zn)),
                                     wout_ref[GM_WIDTH:GM_WIDTH + CV_WIDTH, :], preferred_element_type=F32)

    for b in range(nb):
        pgb = proj(h_ref[b], OFF_DN_GATE, D_PROJ_PAD)
        gate_ref[b * tb:(b + 1) * tb, :] = pgb[:, 0:DN_WIDTH]
        pba_ref[b * tb:(b + 1) * tb, :] = pgb[:, DN_WIDTH:]
        dnpad_ref[b, 0:DN_HALO, :] = dnhalo_ref[b]
        for c in range(3 * DN_HEADS):
            c0, c1 = c * hd, (c + 1) * hd
            cur = dnpad_ref[b, DN_HALO - 8:DN_HALO + tb, c0:c1]
            prev = dnpad_ref[b, DN_HALO - 9:DN_HALO + tb - 1, c0:c1]
            early = dnw_ref[0:1, c0:c1] * prev + dnw_ref[1:2, c0:c1] * cur
            late = dnw_ref[2:3, c0:c1] * prev[8:, :] + dnw_ref[3:4, c0:c1] * cur[8:, :]
            acc = late + early[6:6 + tb, :]
            z = acc * jax.nn.sigmoid(acc)
            if c < 2 * DN_HEADS:
                z = z * lax.rsqrt(jnp.sum(z * z, axis=-1, keepdims=True) + EPS)
                if c < DN_HEADS:
                    z = z * (hd ** -0.5)
            qkv_ref[b * tb:(b + 1) * tb, c0:c1] = z
        dnhalo_ref[b] = dnpad_ref[b, tb:tb + DN_HALO, :]

    pba = pba_ref[...]
    beta_c = jax.nn.sigmoid(pba)
    gam_c = -jnp.exp(alog_ref[...]) * jax.nn.softplus(pba + dtb_ref[...])
    row_in_chunk = lax.broadcasted_iota(jnp.int32, (n_rows, LANES), 0) % CHUNK
    shift = 1
    while shift < CHUNK:
        gam_c = gam_c + jnp.where(row_in_chunk >= shift, pltpu.roll(gam_c, shift, 0), 0.0)
        shift *= 2
    for h in range(DN_HEADS):
        gam_ref[:, h * hd:(h + 1) * hd] = jnp.broadcast_to(
            gam_c[:, DN_HEADS + h:DN_HEADS + h + 1], (n_rows, hd))
        betar_ref[:, h * hd:(h + 1) * hd] = jnp.broadcast_to(beta_c[:, h:h + 1], (n_rows, hd))

    ri = lax.broadcasted_iota(jnp.int32, (CHUNK, LANES), 0)
    lane = lax.broadcasted_iota(jnp.int32, (CHUNK, LANES), 1)
    ci = lane % CHUNK
    first = lane < CHUNK
    incl = ri >= ci
    strict = ri > ci
    same16 = (ri // 16) == (ci // 16)
    same32 = (ri // 32) == (ci // 32)
    m_bd16 = strict & same16
    m_e = strict & same32 & jnp.logical_not(same16)
    m_f = strict & jnp.logical_not(same32)
    zeros_h = jnp.zeros((CHUNK, hd), BF16)

    def pdot(x, y):
        ybd = _cat_rows([_bf(jnp.where(first, y, 0.0)), _bf(jnp.where(first, 0.0, y))])
        return jnp.dot(_bf(x), ybd, preferred_element_type=F32)

    chains = [(g, p) for g in range(nb * cpb) for p in range(DN_HEADS // 2)]
    lane2 = lax.broadcasted_iota(jnp.int32, (CHUNK, pw), 1)
    head0 = lane2 < hd

    def per_chain(fn, *cols):
        return [fn(*args) for args in zip(*cols)]

    def load(ref, col_off):
        return [ref[g * CHUNK:(g + 1) * CHUNK, col_off + p * pw:col_off + (p + 1) * pw] for g, p in chains]

    gr = load(gam_ref, 0)
    br = load(betar_ref, 0)
    q = load(qkv_ref, 0)
    k = load(qkv_ref, DN_WIDTH)

    def decay(gr_c):
        gcol = jnp.where(first, gr_c[:, 0:hd], gr_c[:, hd:pw])
        grow = _cat_rows([gr_c[:, 0:hd], gr_c[:, hd:pw]]).T[0:CHUNK, :]
        return jnp.where(incl, jnp.exp(jnp.minimum(gcol - grow, 0.0)), 0.0)

    dec = per_chain(decay, gr)
    qkkk = per_chain(
        lambda q_c, k_c: _dot_nt(
            _cat_rows([_bf(q_c), _bf(k_c)]),
            _cat_rows([_bf(jnp.where(head0, k_c, 0.0)), _bf(jnp.where(head0, 0.0, k_c))])),
        q, k)
    qkm = per_chain(lambda m, d: _bf(m[0:CHUNK, :] * d), qkkk, dec)
    m0 = per_chain(
        lambda m, d, br_c: -(jnp.where(first, br_c[:, 0:hd], br_c[:, hd:pw]) * m[CHUNK:, :] * d),
        qkkk, dec, br)

    a1 = per_chain(lambda m: jnp.where(m_bd16, m, 0.0), m0)
    a2 = per_chain(pdot, a1, a1)
    a4 = per_chain(pdot, a2, a2)
    a8 = per_chain(pdot, a4, a4)
    x = per_chain(lambda a, b_, c: a + b_ + c, a1, a2, per_chain(pdot, a1, a2))
    x = per_chain(lambda a, b_, c: a + b_ + c, x, a4, per_chain(pdot, x, a4))
    x = per_chain(lambda a, b_, c: a + b_ + c, x, a8, per_chain(pdot, x, a8))
    for mask in (m_e, m_f):
        e = per_chain(lambda m: jnp.where(mask, m, 0.0), m0)
        y = per_chain(lambda a, b_: a + b_, e, per_chain(pdot, x, e))
        x = per_chain(lambda a, b_, c: a + b_ + c, x, y, per_chain(pdot, y, x))

    v = load(qkv_ref, 2 * DN_WIDTH)
    egam = per_chain(jnp.exp, gr)
    bv = per_chain(lambda a, b_: a * b_, br, v)
    bk = per_chain(lambda a, e_, k_c: a * e_ * k_c, br, egam, k)

    def solve(x_c, bv_c, bk_c):
        rhs = _cat_lanes([bv_c[:, 0:hd], bk_c[:, 0:hd], bv_c[:, hd:pw], bk_c[:, hd:pw]])
        rhs_bd = _cat_rows([
            _cat_lanes([_bf(bv_c[:, 0:hd]), _bf(bk_c[:, 0:hd]), zeros_h, zeros_h]),
            _cat_lanes([zeros_h, zeros_h, _bf(bv_c[:, hd:pw]), _bf(bk_c[:, hd:pw])])])
        return rhs + jnp.dot(_bf(x_c), rhs_bd, preferred_element_type=F32)

    sol = per_chain(solve, x, bv, bk)
    u_part = per_chain(lambda s_: _cat_lanes([s_[:, 0:hd], s_[:, 2 * hd:3 * hd]]), sol)
    w_part = per_chain(lambda s_: _bf(_cat_lanes([s_[:, hd:2 * hd], s_[:, 3 * hd:4 * hd]])), sol)
    qd = per_chain(lambda q_c, e_: _bf(q_c * e_), q, egam)
    kd = per_chain(lambda k_c, g_c: _bf(k_c * jnp.exp(g_c[CHUNK - 1:CHUNK, :] - g_c)), k, gr)
    cd = per_chain(lambda g_c: jnp.exp(g_c[CHUNK - 1:CHUNK, :]), gr)

    dng = dng_ref[...]
    zeros_s = jnp.zeros((hd, hd), BF16)
    state = [[s_ref[b * DN_HEADS + h] for h in range(DN_HEADS)] for b in range(nb)]
    out_rows = [[None] * cpb for _ in range(nb)]
    pairs = [(b, p) for b in range(nb) for p in range(DN_HEADS // 2)]
    for j in range(cpb):
        idx = [chains.index((b * cpb + j, p)) for b, p in pairs]
        sbd = [_cat_rows([_cat_lanes([_bf(state[b][2 * p]), zeros_s]),
                          _cat_lanes([zeros_s, _bf(state[b][2 * p + 1])])]) for b, p in pairs]
        ws = [jnp.dot(_cat_rows([w_part[i], qd[i]]), s_, preferred_element_type=F32)
              for i, s_ in zip(idx, sbd)]
        ub = [_bf(u_part[i] - w_[0:CHUNK, :]) for i, w_ in zip(idx, ws)]
        ubd = [_cat_rows([_cat_lanes([u_[:, 0:hd], zeros_h]), _cat_lanes([zeros_h, u_[:, hd:pw]])]) for u_ in ub]
        o = [w_[CHUNK:, :] + jnp.dot(qkm[i], u_, preferred_element_type=F32)
             for i, w_, u_ in zip(idx, ws, ubd)]
        ktu = [[_dot_tn(kd[i][:, hh * hd:(hh + 1) * hd], u_[:, hh * hd:(hh + 1) * hd]) for hh in range(2)]
               for i, u_ in zip(idx, ub)]
        for n, (b, p) in enumerate(pairs):
            for hh in range(2):
                c0, c1 = hh * hd, (hh + 1) * hd
                state[b][2 * p + hh] = state[b][2 * p + hh] * cd[idx[n]][:, c0:c1] + ktu[n][hh]
        for b in range(nb):
            r0 = (b * cpb + j) * CHUNK
            gate = gate_ref[r0:r0 + CHUNK, :]
            o_b = _cat_lanes([o[pairs.index((b, p))] for p in range(DN_HEADS // 2)])
            normed = _cat_lanes([_rms_norm(o_b[:, h * hd:(h + 1) * hd], dng) for h in range(DN_HEADS)])
            out_rows[b][j] = _bf(normed * (gate * jax.nn.sigmoid(gate)))
    s_ref[...] = jnp.stack([state[b][h] for b in range(nb) for h in range(DN_HEADS)], axis=0)
    w_out_c = wout_ref[GM_WIDTH + CV_WIDTH:, :]
    for b in range(nb):
        out_ref[b] += jnp.dot(_cat_rows(out_rows[b]), w_out_c, preferred_element_type=F32)


def _mix(x3d, prm):
    b, t, d = x3d.shape
    nb, tb = NB_MIX, TB_MIX
    n_rows = nb * tb

    def full(a):
        nd = a.ndim
        return pl.BlockSpec(a.shape, lambda i, j, _nd=nd: (0,) * _nd, pipeline_mode=pl.Buffered(1))

    params = [prm["g_in"], prm["w_in"], prm["w_out"],
              prm["gm_ln_g"], prm["gm_ln_b"], prm["gm_wcat"], prm["gm_bias"],
              prm["cv_w"], prm["cv_b"], prm["cv_ln_g"], prm["cv_ln_b"],
              prm["dn_w"], prm["dn_alog"], prm["dn_dtb"], prm["dn_norm_g"]]
    return pl.pallas_call(
        _mix_kernel,
        grid=(b // nb, t // tb),
        in_specs=[pl.BlockSpec((nb, tb, d), lambda i, j: (i, j, 0))] + [full(a) for a in params],
        out_specs=pl.BlockSpec((nb, tb, d), lambda i, j: (i, j, 0)),
        out_shape=jax.ShapeDtypeStruct((b, t, d), F32),
        scratch_shapes=[
            pltpu.VMEM((nb, tb, d), BF16),
            pltpu.VMEM((nb, tb, 2 * GM_WIDTH), F32),
            pltpu.VMEM((nb, tb, 2 * CV_WIDTH), F32),
            pltpu.VMEM((CV_HALO + tb + CV_TAIL, CV_WIDTH), F32),
            pltpu.VMEM((nb, CV_HALO, CV_WIDTH), F32),
            pltpu.VMEM((nb, DN_HALO + tb, 3 * DN_WIDTH), F32),
            pltpu.VMEM((nb, DN_HALO, 3 * DN_WIDTH), F32),
            pltpu.VMEM((n_rows, 3 * DN_WIDTH), F32),
            pltpu.VMEM((n_rows, DN_WIDTH), F32),
            pltpu.VMEM((n_rows, LANES), F32),
            pltpu.VMEM((n_rows, DN_WIDTH), F32),
            pltpu.VMEM((n_rows, DN_WIDTH), F32),
            pltpu.VMEM((nb * DN_HEADS, DN_HEAD_DIM, DN_HEAD_DIM), F32),
        ],
        compiler_params=pltpu.CompilerParams(
            dimension_semantics=("arbitrary", "arbitrary"),
            vmem_limit_bytes=58 * 1024 * 1024),
    )(x3d, *params)


def _pad_rows(a, rows):
    return jnp.pad(a, ((0, rows - a.shape[0]), (0, 0)))


def kernel(x, mix_norm_g, w_in, gm_ln_g, gm_ln_b, gm_ws, gm_bs, cv_dw_w, cv_dw_b, cv_ln_g, cv_ln_b, dn_conv_w, dn_a_log, dn_dt_bias, dn_norm_g, w_out, ffn_norm_g, w_gate, w_up, w_down, final_norm_g):
    b, t, d = x.shape
    depth = w_in.shape[0]
    d_proj = w_in.shape[2]
    d_ff = w_gate.shape[2]
    assert t % TB_MIX == 0 and b % NB_MIX == 0 and TB_MIX == GM_BLOCK and dn_conv_w.shape[1] == DN_CONV == 4
    assert (b * t) % TM_FFN == 0
    assert d_ff % FF_CHUNK == 0 and d_proj == OFF_DN_BA + 2 * DN_HEADS

    for l in range(depth):
        lane_pad = (0, LANES - 2 * DN_HEADS)
        prm = {
            "g_in": mix_norm_g[l][None, :],
            "w_in": jnp.pad(w_in[l], ((0, 0), (0, D_PROJ_PAD - d_proj))).astype(BF16),
            "w_out": w_out[l].astype(BF16),
            "gm_ln_g": gm_ln_g[l][None, :], "gm_ln_b": gm_ln_b[l][None, :],
            "gm_wcat": gm_ws[l].transpose(1, 0, 2).reshape(GM_BLOCK, GM_HEADS * GM_BLOCK),
            "gm_bias": jnp.repeat(gm_bs[l].T, GM_HEAD_DIM, axis=1),
            "cv_w": _pad_rows(cv_dw_w[l], 32), "cv_b": cv_dw_b[l][None, :],
            "cv_ln_g": cv_ln_g[l][None, :], "cv_ln_b": cv_ln_b[l][None, :],
            "dn_w": _pad_rows(dn_conv_w[l], 8),
            "dn_alog": jnp.pad(jnp.concatenate([jnp.zeros((DN_HEADS,), F32), dn_a_log[l]]), lane_pad)[None, :],
            "dn_dtb": jnp.pad(jnp.concatenate([jnp.zeros((DN_HEADS,), F32), dn_dt_bias[l]]), lane_pad)[None, :],
            "dn_norm_g": dn_norm_g[l][None, :],
        }
        x = _mix(x, prm)
        x2d = _ffn(x.reshape(b * t, d), ffn_norm_g[l][None, :], w_gate[l].astype(BF16),
                   w_up[l].astype(BF16), w_down[l].astype(BF16), final_norm_g[None, :],
                   final_norm=(l == depth - 1))
        x = x2d.reshape(b, t, d)
    return x
```

```python
import functools

import jax
import jax.numpy as jnp
from jax import lax
from jax.experimental import pallas as pl
from jax.experimental.pallas import tpu as pltpu

F32 = jnp.float32
BF16 = jnp.bfloat16
EPS = 1e-6

LANES = 128
V7X_VMEM_BYTES = 64 * 1024 * 1024

GM_WIDTH = 256
GM_HEADS = 4
GM_HEAD_DIM = GM_WIDTH // GM_HEADS
GM_BLOCK = 128
CV_WIDTH = 256
CV_KERNEL = 31
DN_WIDTH = 512
DN_HEAD_DIM = 128
DN_HEADS = 4
DN_CONV = 4
CHUNK = 64

OFF_GM_U = 0
OFF_GM_V = OFF_GM_U + GM_WIDTH
OFF_CV_A = OFF_GM_V + GM_WIDTH
OFF_CV_G = OFF_CV_A + CV_WIDTH
OFF_DN_QKV = OFF_CV_G + CV_WIDTH
OFF_DN_GATE = OFF_DN_QKV + 3 * DN_WIDTH
OFF_DN_BA = OFF_DN_GATE + DN_WIDTH
D_PROJ_PAD = OFF_DN_BA + LANES

TM_FFN = 1024
TB_MIX = 128
NB_MIX = 8
FF_CHUNK = 256
CV_HALO = 32
DN_HALO = 16
CV_TAIL = 8


def _rms_norm(x, g):
    ms = jnp.mean(x * x, axis=-1, keepdims=True)
    return x * lax.rsqrt(ms + EPS) * g


def _layer_norm(x, g, b):
    mu = jnp.mean(x, axis=-1, keepdims=True)
    xc = x - mu
    var = jnp.mean(xc * xc, axis=-1, keepdims=True)
    return xc * lax.rsqrt(var + EPS) * g + b


def _gelu(x):
    return 0.5 * x * (1.0 + lax.erf(x * (2.0 ** -0.5)))


def _dot_nt(a, b):
    return lax.dot_general(a.astype(BF16), b.astype(BF16), (((1,), (1,)), ((), ())),
                           preferred_element_type=F32)


def _dot_tn(a, b):
    return lax.dot_general(a.astype(BF16), b.astype(BF16), (((0,), (0,)), ((), ())),
                           preferred_element_type=F32)


def _ffn_kernel(x_ref, g_ref, wg_ref, wu_ref, wd_ref, gf_ref, o_ref, *, final_norm):
    x = x_ref[...]
    h = _rms_norm(x, g_ref[...]).astype(BF16)
    acc = x
    d_ff = wg_ref.shape[1]
    for c0 in range(0, d_ff, FF_CHUNK):
        c1 = c0 + FF_CHUNK
        gate = jnp.dot(h, wg_ref[:, c0:c1], preferred_element_type=F32)
        up = jnp.dot(h, wu_ref[:, c0:c1], preferred_element_type=F32)
        act = (gate * jax.nn.sigmoid(gate) * up).astype(BF16)
        acc = acc + jnp.dot(act, wd_ref[c0:c1, :], preferred_element_type=F32)
    if final_norm:
        acc = _rms_norm(acc, gf_ref[...])
    o_ref[...] = acc


def _ffn(x2d, g, wg, wu, wd, gf, final_norm):
    n_tok, d = x2d.shape
    const2 = lambda i: (0, 0)
    resident = pl.Buffered(1)
    return pl.pallas_call(
        functools.partial(_ffn_kernel, final_norm=final_norm),
        grid=(n_tok // TM_FFN,),
        in_specs=[
            pl.BlockSpec((TM_FFN, d), lambda i: (i, 0)),
            pl.BlockSpec((1, d), const2, pipeline_mode=resident),
            pl.BlockSpec(wg.shape, const2, pipeline_mode=resident),
            pl.BlockSpec(wu.shape, const2, pipeline_mode=resident),
            pl.BlockSpec(wd.shape, const2, pipeline_mode=resident),
            pl.BlockSpec((1, d), const2, pipeline_mode=resident),
        ],
        out_specs=pl.BlockSpec((TM_FFN, d), lambda i: (i, 0)),
        out_shape=jax.ShapeDtypeStruct((n_tok, d), F32),
        compiler_params=pltpu.CompilerParams(
            dimension_semantics=("arbitrary",),
            vmem_limit_bytes=56 * 1024 * 1024),
    )(x2d, g, wg, wu, wd, gf)


def _bf(x):
    return x.astype(BF16)


def _cat_rows(xs):
    return jnp.concatenate(xs, axis=0)


def _cat_lanes(xs):
    return jnp.concatenate(xs, axis=1)


def _mix_kernel(x_ref, g_in_ref, win_ref, wout_ref,
                gm_g_ref, gm_b_ref, wcat_ref, gm_bias_ref,
                cvw_ref, cvb_ref, cvg_ref, cvbeta_ref,
                dnw_ref, alog_ref, dtb_ref, dng_ref,
                out_ref,
                h_ref, pg_ref, pc_ref, ypad_ref, yhalo_ref, dnpad_ref, dnhalo_ref, qkv_ref, gate_ref, pba_ref,
                gam_ref, betar_ref, s_ref):
    t = pl.program_id(1)
    nb, tb = out_ref.shape[0], out_ref.shape[1]
    n_rows = nb * tb
    cpb = tb // CHUNK
    hd = DN_HEAD_DIM
    pw = 2 * hd

    @pl.when(t == 0)
    def _():
        yhalo_ref[...] = jnp.zeros(yhalo_ref.shape, F32)
        dnhalo_ref[...] = jnp.zeros(dnhalo_ref.shape, F32)
        s_ref[...] = jnp.zeros(s_ref.shape, F32)
        ypad_ref[CV_HALO + tb:, :] = jnp.zeros((CV_TAIL, CV_WIDTH), F32)

    row_i = lax.broadcasted_iota(jnp.int32, (GM_BLOCK, GM_HEADS * GM_BLOCK), 0)
    col_i = lax.broadcasted_iota(jnp.int32, (GM_BLOCK, GM_HEADS * GM_BLOCK), 1) % GM_BLOCK
    w_cat = _bf(jnp.where(row_i >= col_i, wcat_ref[...], 0.0))
    head_of_lane = lax.broadcasted_iota(jnp.int32, (GM_BLOCK, GM_WIDTH), 1) // GM_HEAD_DIM
    gm_g = gm_g_ref[...]
    gm_b = gm_b_ref[...]
    cvb = cvb_ref[...]
    cvg = cvg_ref[...]
    cvbeta = cvbeta_ref[...]
    cv_lead = CV_HALO - (CV_KERNEL - 1)
    slab = tb + 8
    g_in = g_in_ref[...]

    def proj(hb, c0, c1):
        return jnp.dot(hb, win_ref[:, c0:c1], preferred_element_type=F32)

    for b in range(nb):
        h_ref[b] = _bf(_rms_norm(x_ref[b], g_in))
        pg_ref[b] = proj(h_ref[b], OFF_GM_U, OFF_CV_A)

    ya = []
    for b in range(nb):
        pc_ref[b] = proj(h_ref[b], OFF_CV_A, OFF_DN_QKV)
        u = _gelu(pg_ref[b, :, 0:GM_WIDTH])
        vn = _layer_norm(_gelu(pg_ref[b, :, GM_WIDTH:]), gm_g, gm_b)
        vbd = _cat_rows([_bf(jnp.where(head_of_lane == h, vn, 0.0)) for h in range(GM_HEADS)])
        s = gm_bias_ref[...] + jnp.dot(w_cat, vbd, preferred_element_type=F32)
        ya.append(_bf(u * s))

    for b in range(nb):
        dnpad_ref[b, DN_HALO:DN_HALO + tb, :] = proj(h_ref[b], OFF_DN_QKV, OFF_DN_GATE)
        acc_x = x_ref[b] + jnp.dot(ya[b], wout_ref[0:GM_WIDTH, :], preferred_element_type=F32)
        ypad_ref[0:CV_HALO, :] = yhalo_ref[b]
        ypad_ref[CV_HALO:CV_HALO + tb, :] = (
            pc_ref[b, :, 0:CV_WIDTH] * jax.nn.sigmoid(pc_ref[b, :, CV_WIDTH:]))
        acc = jnp.broadcast_to(cvb, (tb, CV_WIDTH))
        for r in range(8):
            z = None
            for s_off in range(cv_lead, cv_lead + CV_KERNEL):
                if s_off % 8 != r:
                    continue
                k = s_off - cv_lead
                a8 = (s_off // 8) * 8
                term = cvw_ref[k:k + 1, :] * ypad_ref[a8:a8 + slab, :]
                z = term if z is None else z + term
            if z is not None:
                acc = acc + z[r:r + tb, :]
        yhalo_ref[b] = ypad_ref[tb:tb + CV_HALO, :]
        zn = _layer_norm(acc, cvg, cvbeta)
        out_ref[b] = acc_x + jnp.dot(_bf(zn * jax.nn.sigmoid(zn)),
                                     wout_ref[GM_WIDTH:GM_WIDTH + CV_WIDTH, :], preferred_element_type=F32)

    for b in range(nb):
        pgb = proj(h_ref[b], OFF_DN_GATE, D_PROJ_PAD)
        gate_ref[b * tb:(b + 1) * tb, :] = pgb[:, 0:DN_WIDTH]
        pba_ref[b * tb:(b + 1) * tb, :] = pgb[:, DN_WIDTH:]
        dnpad_ref[b, 0:DN_HALO, :] = dnhalo_ref[b]
        for c in range(3 * DN_HEADS):
            c0, c1 = c * hd, (c + 1) * hd
            cur = dnpad_ref[b, DN_HALO - 8:DN_HALO + tb, c0:c1]
            prev = dnpad_ref[b, DN_HALO - 9:DN_HALO + tb - 1, c0:c1]
            early = dnw_ref[0:1, c0:c1] * prev + dnw_ref[1:2, c0:c1] * cur
            late = dnw_ref[2:3, c0:c1] * prev[8:, :] + dnw_ref[3:4, c0:c1] * cur[8:, :]
            acc = late + early[6:6 + tb, :]
            z = acc * jax.nn.sigmoid(acc)
            if c < 2 * DN_HEADS:
                z = z * lax.rsqrt(jnp.sum(z * z, axis=-1, keepdims=True) + EPS)
                if c < DN_HEADS:
                    z = z * (hd ** -0.5)
            qkv_ref[b * tb:(b + 1) * tb, c0:c1] = z
        dnhalo_ref[b] = dnpad_ref[b, tb:tb + DN_HALO, :]

    pba = pba_ref[...]
    beta_c = jax.nn.sigmoid(pba)
    gam_c = -jnp.exp(alog_ref[...]) * jax.nn.softplus(pba + dtb_ref[...])
    row_in_chunk = lax.broadcasted_iota(jnp.int32, (n_rows, LANES), 0) % CHUNK
    shift = 1
    while shift < CHUNK:
        gam_c = gam_c + jnp.where(row_in_chunk >= shift, pltpu.roll(gam_c, shift, 0), 0.0)
        shift *= 2
    for h in range(DN_HEADS):
        gam_ref[:, h * hd:(h + 1) * hd] = jnp.broadcast_to(
            gam_c[:, DN_HEADS + h:DN_HEADS + h + 1], (n_rows, hd))
        betar_ref[:, h * hd:(h + 1) * hd] = jnp.broadcast_to(beta_c[:, h:h + 1], (n_rows, hd))

    ri = lax.broadcasted_iota(jnp.int32, (CHUNK, LANES), 0)
    lane = lax.broadcasted_iota(jnp.int32, (CHUNK, LANES), 1)
    ci = lane % CHUNK
    first = lane < CHUNK
    incl = ri >= ci
    strict = ri > ci
    same16 = (ri // 16) == (ci // 16)
    same32 = (ri // 32) == (ci // 32)
    m_bd16 = strict & same16
    m_e = strict & same32 & jnp.logical_not(same16)
    m_f = strict & jnp.logical_not(same32)
    zeros_h = jnp.zeros((CHUNK, hd), BF16)

    def pdot(x, y):
        ybd = _cat_rows([_bf(jnp.where(first, y, 0.0)), _bf(jnp.where(first, 0.0, y))])
        return jnp.dot(_bf(x), ybd, preferred_element_type=F32)

    chains = [(g, p) for g in range(nb * cpb) for p in range(DN_HEADS // 2)]
    lane2 = lax.broadcasted_iota(jnp.int32, (CHUNK, pw), 1)
    head0 = lane2 < hd

    def per_chain(fn, *cols):
        return [fn(*args) for args in zip(*cols)]

    def load(ref, col_off):
        return [ref[g * CHUNK:(g + 1) * CHUNK, col_off + p * pw:col_off + (p + 1) * pw] for g, p in chains]

    gr = load(gam_ref, 0)
    br = load(betar_ref, 0)
    q = load(qkv_ref, 0)
    k = load(qkv_ref, DN_WIDTH)

    def decay(gr_c):
        gcol = jnp.where(first, gr_c[:, 0:hd], gr_c[:, hd:pw])
        grow = _cat_rows([gr_c[:, 0:hd], gr_c[:, hd:pw]]).T[0:CHUNK, :]
        return jnp.where(incl, jnp.exp(jnp.minimum(gcol - grow, 0.0)), 0.0)

    dec = per_chain(decay, gr)
    qkkk = per_chain(
        lambda q_c, k_c: _dot_nt(
            _cat_rows([_bf(q_c), _bf(k_c)]),
            _cat_rows([_bf(jnp.where(head0, k_c, 0.0)), _bf(jnp.where(head0, 0.0, k_c))])),
        q, k)
    qkm = per_chain(lambda m, d: _bf(m[0:CHUNK, :] * d), qkkk, dec)
    m0 = per_chain(
        lambda m, d, br_c: -(jnp.where(first, br_c[:, 0:hd], br_c[:, hd:pw]) * m[CHUNK:, :] * d),
        qkkk, dec, br)

    a1 = per_chain(lambda m: jnp.where(m_bd16, m, 0.0), m0)
    a2 = per_chain(pdot, a1, a1)
    a4 = per_chain(pdot, a2, a2)
    a8 = per_chain(pdot, a4, a4)
    x = per_chain(lambda a, b_, c: a + b_ + c, a1, a2, per_chain(pdot, a1, a2))
    x = per_chain(lambda a, b_, c: a + b_ + c, x, a4, per_chain(pdot, x, a4))
    x = per_chain(lambda a, b_, c: a + b_ + c, x, a8, per_chain(pdot, x, a8))
    for mask in (m_e, m_f):
        e = per_chain(lambda m: jnp.where(mask, m, 0.0), m0)
        y = per_chain(lambda a, b_: a + b_, e, per_chain(pdot, x, e))
        x = per_chain(lambda a, b_, c: a + b_ + c, x, y, per_chain(pdot, y, x))

    v = load(qkv_ref, 2 * DN_WIDTH)
    egam = per_chain(jnp.exp, gr)
    bv = per_chain(lambda a, b_: a * b_, br, v)
    bk = per_chain(lambda a, e_, k_c: a * e_ * k_c, br, egam, k)

    def solve(x_c, bv_c, bk_c):
        rhs = _cat_lanes([bv_c[:, 0:hd], bk_c[:, 0:hd], bv_c[:, hd:pw], bk_c[:, hd:pw]])
        rhs_bd = _cat_rows([
            _cat_lanes([_bf(bv_c[:, 0:hd]), _bf(bk_c[:, 0:hd]), zeros_h, zeros_h]),
            _cat_lanes([zeros_h, zeros_h, _bf(bv_c[:, hd:pw]), _bf(bk_c[:, hd:pw])])])
        return rhs + jnp.dot(_bf(x_c), rhs_bd, preferred_element_type=F32)

    sol = per_chain(solve, x, bv, bk)
    u_part = per_chain(lambda s_: _cat_lanes([s_[:, 0:hd], s_[:, 2 * hd:3 * hd]]), sol)
    w_part = per_chain(lambda s_: _bf(_cat_lanes([s_[:, hd:2 * hd], s_[:, 3 * hd:4 * hd]])), sol)
    qd = per_chain(lambda q_c, e_: _bf(q_c * e_), q, egam)
    kd = per_chain(lambda k_c, g_c: _bf(k_c * jnp.exp(g_c[CHUNK - 1:CHUNK, :] - g_c)), k, gr)
    cd = per_chain(lambda g_c: jnp.exp(g_c[CHUNK - 1:CHUNK, :]), gr)

    dng = dng_ref[...]
    zeros_s = jnp.zeros((hd, hd), BF16)
    state = [[s_ref[b * DN_HEADS + h] for h in range(DN_HEADS)] for b in range(nb)]
    out_rows = [[None] * cpb for _ in range(nb)]
    pairs = [(b, p) for b in range(nb) for p in range(DN_HEADS // 2)]
    for j in range(cpb):
        idx = [chains.index((b * cpb + j, p)) for b, p in pairs]
        sbd = [_cat_rows([_cat_lanes([_bf(state[b][2 * p]), zeros_s]),
                          _cat_lanes([zeros_s, _bf(state[b][2 * p + 1])])]) for b, p in pairs]
        ws = [jnp.dot(_cat_rows([w_part[i], qd[i]]), s_, preferred_element_type=F32)
              for i, s_ in zip(idx, sbd)]
        ub = [_bf(u_part[i] - w_[0:CHUNK, :]) for i, w_ in zip(idx, ws)]
        ubd = [_cat_rows([_cat_lanes([u_[:, 0:hd], zeros_h]), _cat_lanes([zeros_h, u_[:, hd:pw]])]) for u_ in ub]
        o = [w_[CHUNK:, :] + jnp.dot(qkm[i], u_, preferred_element_type=F32)
             for i, w_, u_ in zip(idx, ws, ubd)]
        ktu = [[_dot_tn(kd[i][:, hh * hd:(hh + 1) * hd], u_[:, hh * hd:(hh + 1) * hd]) for hh in range(2)]
               for i, u_ in zip(idx, ub)]
        for n, (b, p) in enumerate(pairs):
            for hh in range(2):
                c0, c1 = hh * hd, (hh + 1) * hd
                state[b][2 * p + hh] = state[b][2 * p + hh] * cd[idx[n]][:, c0:c1] + ktu[n][hh]
        for b in range(nb):
            r0 = (b * cpb + j) * CHUNK
            gate = gate_ref[r0:r0 + CHUNK, :]
            o_b = _cat_lanes([o[pairs.index((b, p))] for p in range(DN_HEADS // 2)])
            normed = _cat_lanes([_rms_norm(o_b[:, h * hd:(h + 1) * hd], dng) for h in range(DN_HEADS)])
            out_rows[b][j] = _bf(normed * (gate * jax.nn.sigmoid(gate)))
    s_ref[...] = jnp.stack([state[b][h] for b in range(nb) for h in range(DN_HEADS)], axis=0)
    w_out_c = wout_ref[GM_WIDTH + CV_WIDTH:, :]
    for b in range(nb):
        out_ref[b] += jnp.dot(_cat_rows(out_rows[b]), w_out_c, preferred_element_type=F32)


def _mix(x3d, prm):
    b, t, d = x3d.shape
    nb, tb = NB_MIX, TB_MIX
    n_rows = nb * tb

    def full(a):
        nd = a.ndim
        return pl.BlockSpec(a.shape, lambda i, j, _nd=nd: (0,) * _nd, pipeline_mode=pl.Buffered(1))

    params = [prm["g_in"], prm["w_in"], prm["w_out"],
              prm["gm_ln_g"], prm["gm_ln_b"], prm["gm_wcat"], prm["gm_bias"],
              prm["cv_w"], prm["cv_b"], prm["cv_ln_g"], prm["cv_ln_b"],
              prm["dn_w"], prm["dn_alog"], prm["dn_dtb"], prm["dn_norm_g"]]
    return pl.pallas_call(
        _mix_kernel,
        grid=(b // nb, t // tb),
        in_specs=[pl.BlockSpec((nb, tb, d), lambda i, j: (i, j, 0))] + [full(a) for a in params],
        out_specs=pl.BlockSpec((nb, tb, d), lambda i, j: (i, j, 0)),
        out_shape=jax.ShapeDtypeStruct((b, t, d), F32),
        scratch_shapes=[
            pltpu.VMEM((nb, tb, d), BF16),
            pltpu.VMEM((nb, tb, 2 * GM_WIDTH), F32),
            pltpu.VMEM((nb, tb, 2 * CV_WIDTH), F32),
            pltpu.VMEM((CV_HALO + tb + CV_TAIL, CV_WIDTH), F32),
            pltpu.VMEM((nb, CV_HALO, CV_WIDTH), F32),
            pltpu.VMEM((nb, DN_HALO + tb, 3 * DN_WIDTH), F32),
            pltpu.VMEM((nb, DN_HALO, 3 * DN_WIDTH), F32),
            pltpu.VMEM((n_rows, 3 * DN_WIDTH), F32),
            pltpu.VMEM((n_rows, DN_WIDTH), F32),
            pltpu.VMEM((n_rows, LANES), F32),
            pltpu.VMEM((n_rows, DN_WIDTH), F32),
            pltpu.VMEM((n_rows, DN_WIDTH), F32),
            pltpu.VMEM((nb * DN_HEADS, DN_HEAD_DIM, DN_HEAD_DIM), F32),
        ],
        compiler_params=pltpu.CompilerParams(
            dimension_semantics=("arbitrary", "arbitrary"),
            vmem_limit_bytes=58 * 1024 * 1024),
    )(x3d, *params)


def _pad_rows(a, rows):
    return jnp.pad(a, ((0, rows - a.shape[0]), (0, 0)))


def kernel(x, mix_norm_g, w_in, gm_ln_g, gm_ln_b, gm_ws, gm_bs, cv_dw_w, cv_dw_b, cv_ln_g, cv_ln_b, dn_conv_w, dn_a_log, dn_dt_bias, dn_norm_g, w_out, ffn_norm_g, w_gate, w_up, w_down, final_norm_g):
    b, t, d = x.shape
    depth = w_in.shape[0]
    d_proj = w_in.shape[2]
    d_ff = w_gate.shape[2]
    assert t % TB_MIX == 0 and b % NB_MIX == 0 and TB_MIX == GM_BLOCK and dn_conv_w.shape[1] == DN_CONV == 4
    assert (b * t) % TM_FFN == 0
    assert d_ff % FF_CHUNK == 0 and d_proj == OFF_DN_BA + 2 * DN_HEADS

    for l in range(depth):
        lane_pad = (0, LANES - 2 * DN_HEADS)
        prm = {
            "g_in": mix_norm_g[l][None, :],
            "w_in": jnp.pad(w_in[l], ((0, 0), (0, D_PROJ_PAD - d_proj))).astype(BF16),
            "w_out": w_out[l].astype(BF16),
            "gm_ln_g": gm_ln_g[l][None, :], "gm_ln_b": gm_ln_b[l][None, :],
            "gm_wcat": gm_ws[l].transpose(1, 0, 2).reshape(GM_BLOCK, GM_HEADS * GM_BLOCK),
            "gm_bias": jnp.repeat(gm_bs[l].T, GM_HEAD_DIM, axis=1),
            "cv_w": _pad_rows(cv_dw_w[l], 32), "cv_b": cv_dw_b[l][None, :],
            "cv_ln_g": cv_ln_g[l][None, :], "cv_ln_b": cv_ln_b[l][None, :],
            "dn_w": _pad_rows(dn_conv_w[l], 8),
            "dn_alog": jnp.pad(jnp.concatenate([jnp.zeros((DN_HEADS,), F32), dn_a_log[l]]), lane_pad)[None, :],
            "dn_dtb": jnp.pad(jnp.concatenate([jnp.zeros((DN_HEADS,), F32), dn_dt_bias[l]]), lane_pad)[None, :],
            "dn_norm_g": dn_norm_g[l][None, :],
        }
        x = _mix(x, prm)
        x2d = _ffn(x.reshape(b * t, d), ffn_norm_g[l][None, :], w_gate[l].astype(BF16),
                   w_up[l].astype(BF16), w_down[l].astype(BF16), final_norm_g[None, :],
                   final_norm=(l == depth - 1))
        x = x2d.reshape(b, t, d)
    return x
```
